```python
import math
import jax
import jax.numpy as jnp
from jax import lax
import numpy as np

D_MODEL = 1024
BATCH = 8
SEQ = 8192
DEPTH = 2

N_HEADS = 16
HEAD_DIM = D_MODEL // N_HEADS
D_FF = 4 * D_MODEL
CHUNK = 64
LEFT_CHUNKS = 8
BAND = (LEFT_CHUNKS + 1) * CHUNK
REL_LEFT = 128
REL_RIGHT = CHUNK - 1
REL_SIZE = REL_LEFT + REL_RIGHT + 1
SB_BLOCK = 128
N_A = DEPTH // 2
N_B = DEPTH - N_A
EPS = 1e-6
NEG_INF = -1e30

kernel_name = "yoco_chunked_relpos_stickbreaking_trunk"


def rmsnorm(x, g):
    xf = x.astype(jnp.float32)
    y = xf * lax.rsqrt(jnp.mean(xf * xf, axis=-1, keepdims=True) + EPS)
    return (y * g.astype(jnp.float32)).astype(x.dtype)


def squared_relu_mlp(h, w_up, w_down):
    a = jax.nn.relu(h @ w_up)
    return (a * a) @ w_down


def chunked_relpos_attention(h, w_qkv, rel_bias, w_o):
    b, s, _ = h.shape
    n_chunks = s // CHUNK
    qkv = h @ w_qkv
    q, k, v = jnp.split(qkv, 3, axis=-1)
    q = q.reshape(b, s, N_HEADS, HEAD_DIM)
    k = k.reshape(b, s, N_HEADS, HEAD_DIM)
    v = v.reshape(b, s, N_HEADS, HEAD_DIM)
    pad = LEFT_CHUNKS * CHUNK
    k_pad = jnp.pad(k, ((0, 0), (pad, 0), (0, 0), (0, 0)))
    v_pad = jnp.pad(v, ((0, 0), (pad, 0), (0, 0), (0, 0)))
    qi = jnp.arange(CHUNK, dtype=jnp.int32)
    kj = jnp.arange(BAND, dtype=jnp.int32)
    rel = qi[:, None] + pad - kj[None, :]
    idx = jnp.clip(rel, -REL_RIGHT, REL_LEFT) + REL_RIGHT
    bias = rel_bias.astype(jnp.float32)[:, idx]
    scale = 1.0 / math.sqrt(HEAD_DIM)
    q_chunks = q.reshape(b, n_chunks, CHUNK, N_HEADS, HEAD_DIM).transpose(1, 0, 2, 3, 4)

    def one_chunk(args):
        c, qc = args
        start = c * CHUNK
        kb = lax.dynamic_slice_in_dim(k_pad, start, BAND, axis=1)
        vb = lax.dynamic_slice_in_dim(v_pad, start, BAND, axis=1)
        sc = jnp.einsum('bqhd,bkhd->bhqk', qc, kb).astype(jnp.float32) * scale + bias[None]
        valid = (start - pad + kj) >= 0
        sc = jnp.where(valid[None, None, None, :], sc, NEG_INF)
        p = jax.nn.softmax(sc, axis=-1).astype(vb.dtype)
        return jnp.einsum('bhqk,bkhd->bqhd', p, vb)

    out = lax.map(one_chunk, (jnp.arange(n_chunks, dtype=jnp.int32), q_chunks))
    out = out.transpose(1, 0, 2, 3, 4).reshape(b, s, D_MODEL)
    return out @ w_o


def stick_breaking_attention(h, w_q, k, v, w_o):
    b, s, _ = h.shape
    q = (h @ w_q).reshape(b, s, N_HEADS, HEAD_DIM)
    scale = 1.0 / math.sqrt(HEAD_DIM)
    outs = []
    for blk in range(s // SB_BLOCK):
        q0 = blk * SB_BLOCK
        k_end = q0 + SB_BLOCK
        qb = q[:, q0:k_end]
        kb = k[:, :k_end]
        vb = v[:, :k_end]
        z = jnp.einsum('bqhd,bkhd->bhqk', qb, kb).astype(jnp.float32) * scale
        t_pos = q0 + jnp.arange(SB_BLOCK, dtype=jnp.int32)
        s_pos = jnp.arange(k_end, dtype=jnp.int32)
        causal = (s_pos[None, :] < t_pos[:, None])[None, None]
        log_keep = jnp.where(causal, jax.nn.log_sigmoid(-z), 0.0)
        later = lax.cumsum(log_keep, axis=3, reverse=True) - log_keep
        a = jnp.where(causal, jnp.exp(jax.nn.log_sigmoid(z) + later), 0.0)
        outs.append(jnp.einsum('bhqk,bkhd->bqhd', a.astype(vb.dtype), vb))
    out = jnp.concatenate(outs, axis=1).reshape(b, s, D_MODEL)
    return out @ w_o


def setup_inputs(seed: int = 0) -> dict:
    key = jax.random.key(seed)
    ks = jax.random.split(key, 20)
    f32 = jnp.float32

    def dense(k, shape, fan_in):
        return jax.random.normal(k, shape, f32) * (fan_in ** -0.5)

    def gain(k, shape):
        return 1.0 + 0.05 * jax.random.normal(k, shape, f32)

    return {
        "x": jax.random.normal(ks[0], (BATCH, SEQ, D_MODEL), f32),
        "a_norm_attn": gain(ks[1], (N_A, D_MODEL)),
        "a_w_qkv": dense(ks[2], (N_A, D_MODEL, 3 * D_MODEL), D_MODEL),
        "a_rel_bias": 0.1 * jax.random.normal(ks[3], (N_A, N_HEADS, REL_SIZE), f32),
        "a_w_o": dense(ks[4], (N_A, D_MODEL, D_MODEL), D_MODEL),
        "a_norm_mlp": gain(ks[5], (N_A, D_MODEL)),
        "a_w_up": dense(ks[6], (N_A, D_MODEL, D_FF), D_MODEL),
        "a_w_down": dense(ks[7], (N_A, D_FF, D_MODEL), D_FF),
        "kv_norm": gain(ks[8], (D_MODEL,)),
        "w_kv": dense(ks[9], (D_MODEL, 2 * D_MODEL), D_MODEL),
        "b_norm_attn": gain(ks[10], (N_B, D_MODEL)),
        "b_w_q": dense(ks[11], (N_B, D_MODEL, D_MODEL), D_MODEL),
        "b_w_o": dense(ks[12], (N_B, D_MODEL, D_MODEL), D_MODEL),
        "b_norm_mlp": gain(ks[13], (N_B, D_MODEL)),
        "b_w_up": dense(ks[14], (N_B, D_MODEL, D_FF), D_MODEL),
        "b_w_down": dense(ks[15], (N_B, D_FF, D_MODEL), D_FF),
        "final_norm": gain(ks[16], (D_MODEL,)),
    }


def reference(x, a_norm_attn, a_w_qkv, a_rel_bias, a_w_o, a_norm_mlp, a_w_up, a_w_down,
              kv_norm, w_kv, b_norm_attn, b_w_q, b_w_o, b_norm_mlp, b_w_up, b_w_down,
              final_norm):
    b, s, _ = x.shape
    h = x
    k_shared = None
    v_shared = None
    for layer in range(DEPTH):
        if layer < N_A:
            i = layer
            h = h + chunked_relpos_attention(rmsnorm(h, a_norm_attn[i]), a_w_qkv[i],
                                             a_rel_bias[i], a_w_o[i])
            h = h + squared_relu_mlp(rmsnorm(h, a_norm_mlp[i]), a_w_up[i], a_w_down[i])
            if layer == N_A - 1:
                kv = rmsnorm(h, kv_norm) @ w_kv
                k_s, v_s = jnp.split(kv, 2, axis=-1)
                k_shared = k_s.reshape(b, s, N_HEADS, HEAD_DIM)
                v_shared = v_s.reshape(b, s, N_HEADS, HEAD_DIM)
        else:
            i = layer - N_A
            h = h + stick_breaking_attention(rmsnorm(h, b_norm_attn[i]), b_w_q[i],
                                             k_shared, v_shared, b_w_o[i])
            h = h + squared_relu_mlp(rmsnorm(h, b_norm_mlp[i]), b_w_up[i], b_w_down[i])
    return rmsnorm(h, final_norm)
```

```python
import functools
import math

import jax
import jax.numpy as jnp
from jax import lax
from jax.experimental import pallas as pl
from jax.experimental.pallas import tpu as pltpu

D_MODEL = 1024
N_HEADS = 16
HEAD_DIM = D_MODEL // N_HEADS
D_FF = 4 * D_MODEL
CHUNK = 64
LEFT_CHUNKS = 8
LEFT_KEYS = LEFT_CHUNKS * CHUNK
REL_LEFT = 128
REL_RIGHT = CHUNK - 1
EPS = 1e-6
NEG_INF = -1e30

V7X_LANES = 128
V7X_MXU_DIM = 256
HEADS_PER_LANE_TILE = V7X_LANES // HEAD_DIM
N_HEAD_PAIRS = N_HEADS // HEADS_PER_LANE_TILE

TOKEN_BLOCK = 512
ATTN_BLOCK = V7X_MXU_DIM
ATTN_A_WINDOW = LEFT_KEYS + ATTN_BLOCK
FF_CHUNK = 1024

F32 = jnp.float32
BF16 = jnp.bfloat16
MIB = 1024 * 1024


def _params(vmem_mib, n_grid):
    return pltpu.CompilerParams(
        dimension_semantics=("arbitrary",) * n_grid,
        vmem_limit_bytes=vmem_mib * MIB,
    )


def _resident(shape, index_map):
    return pl.BlockSpec(shape, index_map, pipeline_mode=pl.Buffered(1))


def _inv_rms(x):
    return lax.rsqrt(jnp.mean(x * x, axis=-1, keepdims=True) + EPS)


def _proj_kernel(x_ref, gq_ref, gkv_ref, wq_ref, wkt_ref, wv_ref, q_ref, kt_ref, v_ref, *,
                 q_scale, shared_norm):
    x = x_ref[...]
    xn = x * _inv_rms(x)
    n_kv = (xn * gkv_ref[...]).astype(BF16)
    n_q = n_kv if shared_norm else (xn * gq_ref[...]).astype(BF16)
    q = jnp.dot(n_q, wq_ref[...], preferred_element_type=F32)
    q_ref[...] = (q * q_scale).astype(BF16)
    v_ref[...] = jnp.dot(n_kv, wv_ref[...], preferred_element_type=F32).astype(BF16)
    kt = lax.dot_general(wkt_ref[...], n_kv, (((1,), (1,)), ((), ())), preferred_element_type=F32)
    kt_ref[...] = kt.astype(BF16)


def _project(x, g_q, g_kv, wq, wkt, wv, *, q_scale, shared_norm):
    b, s, d = x.shape
    tm = TOKEN_BLOCK
    kern = functools.partial(_proj_kernel, q_scale=q_scale, shared_norm=shared_norm)
    wspec = _resident((d, d), lambda i, j: (0, 0))
    gspec = _resident((1, d), lambda i, j: (0, 0))
    return pl.pallas_call(
        kern,
        grid=(b, s // tm),
        in_specs=[pl.BlockSpec((None, tm, d), lambda i, j: (i, j, 0)), gspec, gspec, wspec, wspec, wspec],
        out_specs=[pl.BlockSpec((None, tm, d), lambda i, j: (i, j, 0)),
                   pl.BlockSpec((None, d, tm), lambda i, j: (i, 0, j)),
                   pl.BlockSpec((None, tm, d), lambda i, j: (i, j, 0))],
        out_shape=[jax.ShapeDtypeStruct((b, s, d), BF16),
                   jax.ShapeDtypeStruct((b, d, s), BF16),
                   jax.ShapeDtypeStruct((b, s, d), BF16)],
        compiler_params=_params(40, 2),
        name="proj",
    )(x, g_q, g_kv, wq, wkt, wv)


def _attn_a_kernel(q_ref, kt_ref, v_ref, tb_ref, o_ref):
    tq, win = ATTN_BLOCK, ATTN_A_WINDOW
    q0 = pl.program_id(2) * tq
    ws = jnp.maximum(q0 - LEFT_KEYS, 0)
    coff = pl.multiple_of(LEFT_KEYS - (q0 - ws), tq)
    ws = pl.multiple_of(ws, tq)
    q2 = q_ref[...]
    kt = kt_ref[:, pl.ds(ws, win)]
    v = v_ref[pl.ds(ws, win), :]
    lane = lax.broadcasted_iota(jnp.int32, (tq, V7X_LANES), 1)
    outs = []
    for h in range(HEADS_PER_LANE_TILE):
        in_head = (lane >= h * HEAD_DIM) & (lane < (h + 1) * HEAD_DIM)
        qh = jnp.where(in_head, q2, jnp.zeros_like(q2))
        s = jnp.dot(qh, kt, preferred_element_type=F32) + tb_ref[h, :, pl.ds(coff, win)]
        m = jnp.max(s, axis=-1, keepdims=True)
        p = jnp.exp(s - m)
        l = jnp.sum(p, axis=-1, keepdims=True)
        outs.append(jnp.dot(p.astype(BF16), v, preferred_element_type=F32) / l)
    o_ref[...] = jnp.where(lane < HEAD_DIM, outs[0], outs[1]).astype(BF16)


def _bias_table(rel_bias):
    tq = ATTN_BLOCK
    qi = jnp.arange(tq, dtype=jnp.int32)[:, None]
    dk = jnp.arange(LEFT_KEYS + ATTN_A_WINDOW, dtype=jnp.int32)[None, :] - LEFT_KEYS
    rel = qi - dk
    chunk_gap = qi // CHUNK - jnp.floor_divide(dk, CHUNK)
    in_band = (chunk_gap >= 0) & (chunk_gap <= LEFT_CHUNKS)
    idx = jnp.clip(rel, -REL_RIGHT, REL_LEFT) + REL_RIGHT
    return jnp.where(in_band[None], rel_bias.astype(F32)[:, idx], NEG_INF)


def _attention_a(q, kt, v, table):
    b, s, d = q.shape
    tq = ATTN_BLOCK
    hp = HEADS_PER_LANE_TILE
    return pl.pallas_call(
        _attn_a_kernel,
        grid=(b, N_HEAD_PAIRS, s // tq),
        in_specs=[pl.BlockSpec((None, tq, V7X_LANES), lambda i, p, j: (i, j, p)),
                  pl.BlockSpec((None, V7X_LANES, s), lambda i, p, j: (i, p, 0)),
                  pl.BlockSpec((None, s, V7X_LANES), lambda i, p, j: (i, 0, p)),
                  pl.BlockSpec((hp, tq, table.shape[-1]), lambda i, p, j: (p, 0, 0))],
        out_specs=pl.BlockSpec((None, tq, V7X_LANES), lambda i, p, j: (i, j, p)),
        out_shape=jax.ShapeDtypeStruct((b, s, d), BF16),
        compiler_params=_params(40, 3),
        name="attn_a",
    )(q, kt, v, table)


def _attn_b_kernel(q_ref, kt_ref, v_ref, ntri_ref, o_ref, acc_ref, run_ref):
    t = ATTN_BLOCK
    j = pl.program_id(2)
    q2 = q_ref[...]
    lane = lax.broadcasted_iota(jnp.int32, (t, V7X_LANES), 1)
    qh = [jnp.where((lane >= h * HEAD_DIM) & (lane < (h + 1) * HEAD_DIM), q2, jnp.zeros_like(q2))
          for h in range(HEADS_PER_LANE_TILE)]
    ntri = ntri_ref[...]
    acc_ref[...] = jnp.zeros_like(acc_ref)
    run_ref[...] = jnp.zeros_like(run_ref)

    def tile(kt_idx, diagonal):
        ks = pl.multiple_of(kt_idx * t, t)
        kt = kt_ref[:, pl.ds(ks, t)]
        v = v_ref[pl.ds(ks, t), :]
        if diagonal:
            causal = (lax.broadcasted_iota(jnp.int32, (t, t), 1)
                      < lax.broadcasted_iota(jnp.int32, (t, t), 0))
        for h in range(HEADS_PER_LANE_TILE):
            z = jnp.dot(qh[h], kt, preferred_element_type=F32)
            sp = jnp.maximum(z, 0.0) + jnp.log2(1.0 + jnp.exp2(-jnp.abs(z)))
            if diagonal:
                sp = jnp.where(causal, sp, 0.0)
            hi = sp.astype(BF16)
            lo = (sp - hi.astype(F32)).astype(BF16)
            later = (jnp.dot(hi, ntri, preferred_element_type=F32)
                     + jnp.dot(lo, ntri, preferred_element_type=F32))
            run = run_ref[h]
            a = jnp.exp2(z - sp + later + run)
            if diagonal:
                a = jnp.where(causal, a, 0.0)
            acc_ref[h] += jnp.dot(a.astype(BF16), v, preferred_element_type=F32)
            run_ref[h] = run + later[:, 0:1] - sp[:, 0:1]

    tile(j, True)

    def body(i, carry):
        tile(j - 1 - i, False)
        return carry

    lax.fori_loop(0, j, body, 0)
    o_ref[...] = jnp.where(lane < HEAD_DIM, acc_ref[0], acc_ref[1]).astype(BF16)


def _attention_b(q, kt, v):
    b, s, d = q.shape
    t = ATTN_BLOCK
    hp = HEADS_PER_LANE_TILE
    row = jnp.arange(t, dtype=jnp.int32)[:, None]
    col = jnp.arange(t, dtype=jnp.int32)[None, :]
    ntri = jnp.where(row > col, -1.0, 0.0).astype(BF16)
    return pl.pallas_call(
        _attn_b_kernel,
        grid=(b, N_HEAD_PAIRS, s // t),
        in_specs=[pl.BlockSpec((None, t, V7X_LANES), lambda i, p, j: (i, j, p)),
                  pl.BlockSpec((None, V7X_LANES, s), lambda i, p, j: (i, p, 0)),
                  pl.BlockSpec((None, s, V7X_LANES), lambda i, p, j: (i, 0, p)),
                  _resident((t, t), lambda i, p, j: (0, 0))],
        out_specs=pl.BlockSpec((None, t, V7X_LANES), lambda i, p, j: (i, j, p)),
        out_shape=jax.ShapeDtypeStruct((b, s, d), BF16),
        scratch_shapes=[pltpu.VMEM((hp, t, V7X_LANES), F32), pltpu.VMEM((hp, t, 1), F32)],
        compiler_params=_params(40, 3),
        name="attn_b",
    )(q, kt, v, ntri)


def _mlp_kernel(h_ref, a_ref, wo_ref, g_ref, wup_ref, wdn_ref, gf_ref, o_ref, *, final_norm):
    h1 = h_ref[...] + jnp.dot(a_ref[...], wo_ref[...], preferred_element_type=F32)
    n = (h1 * _inv_rms(h1) * g_ref[...]).astype(BF16)
    acc = h1
    for c in range(D_FF // FF_CHUNK):
        u = jnp.dot(n, wup_ref[:, c * FF_CHUNK:(c + 1) * FF_CHUNK], preferred_element_type=F32)
        r = jnp.maximum(u, 0.0)
        acc = acc + jnp.dot((r * r).astype(BF16), wdn_ref[c * FF_CHUNK:(c + 1) * FF_CHUNK, :],
                            preferred_element_type=F32)
    if final_norm:
        acc = acc * _inv_rms(acc) * gf_ref[...]
    o_ref[...] = acc


def _mlp(h, attn, wo, g, wup, wdn, g_final, *, final_norm):
    n_tok, d = h.shape
    tm = TOKEN_BLOCK
    kern = functools.partial(_mlp_kernel, final_norm=final_norm)
    gspec = _resident((1, d), lambda i: (0, 0))
    return pl.pallas_call(
        kern,
        grid=(n_tok // tm,),
        in_specs=[pl.BlockSpec((tm, d), lambda i: (i, 0)),
                  pl.BlockSpec((tm, d), lambda i: (i, 0)),
                  _resident((d, d), lambda i: (0, 0)),
                  gspec,
                  _resident((d, D_FF), lambda i: (0, 0)),
                  _resident((D_FF, d), lambda i: (0, 0)),
                  gspec],
        out_specs=pl.BlockSpec((tm, d), lambda i: (i, 0)),
        out_shape=jax.ShapeDtypeStruct((n_tok, d), F32),
        compiler_params=_params(56, 1),
        name="mlp_final" if final_norm else "mlp",
    )(h, attn, wo, g, wup, wdn, g_final)


def kernel(x, a_norm_attn, a_w_qkv, a_rel_bias, a_w_o, a_norm_mlp, a_w_up, a_w_down, kv_norm, w_kv,
           b_norm_attn, b_w_q, b_w_o, b_norm_mlp, b_w_up, b_w_down, final_norm):
    b, s, d = x.shape
    assert (s % TOKEN_BLOCK, s % ATTN_BLOCK, d) == (0, 0, D_MODEL) and s >= ATTN_A_WINDOW
    assert a_norm_attn.shape[0] == 1 and b_norm_attn.shape[0] == 1
    scale = 1.0 / math.sqrt(HEAD_DIM)
    row = lambda g: g.reshape(1, d).astype(F32)

    wq, wk, wv = (a_w_qkv[0][:, i * d:(i + 1) * d] for i in range(3))
    g_a = row(a_norm_attn[0])
    q, kt, v = _project(x, g_a, g_a, wq.astype(BF16), wk.T.astype(BF16), wv.astype(BF16),
                        q_scale=scale, shared_norm=True)
    attn = _attention_a(q, kt, v, _bias_table(a_rel_bias[0]))
    h = _mlp(x.reshape(b * s, d), attn.reshape(b * s, d), a_w_o[0].astype(BF16), row(a_norm_mlp[0]),
             a_w_up[0].astype(BF16), a_w_down[0].astype(BF16), row(final_norm), final_norm=False)

    q, kt, v = _project(h.reshape(b, s, d), row(b_norm_attn[0]), row(kv_norm),
                        b_w_q[0].astype(BF16), w_kv[:, :d].T.astype(BF16), w_kv[:, d:].astype(BF16),
                        q_scale=scale * math.log2(math.e), shared_norm=False)
    attn = _attention_b(q, kt, v)
    out = _mlp(h, attn.reshape(b * s, d), b_w_o[0].astype(BF16), row(b_norm_mlp[0]),
               b_w_up[0].astype(BF16), b_w_down[0].astype(BF16), row(final_norm), final_norm=True)
    return out.reshape(b, s, d)
```

```python
import functools
import math

import jax
import jax.numpy as jnp
import numpy as np
from jax import lax
from jax.experimental import pallas as pl
from jax.experimental.pallas import tpu as pltpu

D_MODEL = 1024
N_HEADS = 16
HEAD_DIM = D_MODEL // N_HEADS
D_FF = 4 * D_MODEL
CHUNK = 64
LEFT_CHUNKS = 8
LEFT_KEYS = LEFT_CHUNKS * CHUNK
REL_LEFT = 128
REL_RIGHT = CHUNK - 1
EPS = 1e-6
NEG_INF = -1e30

V7X_LANES = 128
V7X_MXU_DIM = 256
HEADS_PER_LANE_TILE = V7X_LANES // HEAD_DIM
N_HEAD_PAIRS = N_HEADS // HEADS_PER_LANE_TILE

TOKEN_BLOCK = 512
ATTN_BLOCK = V7X_MXU_DIM
ATTN_A_WINDOW = LEFT_KEYS + ATTN_BLOCK
FF_CHUNK = 1024

F32 = jnp.float32
BF16 = jnp.bfloat16
MIB = 1024 * 1024


def _params(vmem_mib, n_grid):
    return pltpu.CompilerParams(
        dimension_semantics=("arbitrary",) * n_grid,
        vmem_limit_bytes=vmem_mib * MIB,
    )


def _resident(shape, index_map):
    return pl.BlockSpec(shape, index_map, pipeline_mode=pl.Buffered(1))


def _inv_rms(x):
    return lax.rsqrt(jnp.mean(x * x, axis=-1, keepdims=True) + EPS)


def _proj_kernel(x_ref, gq_ref, gkv_ref, wq_ref, wkt_ref, wv_ref, q_ref, kt_ref, v_ref, *,
                 q_scale, shared_norm):
    x = x_ref[...]
    xn = x * _inv_rms(x)
    n_kv = (xn * gkv_ref[...]).astype(BF16)
    n_q = n_kv if shared_norm else (xn * gq_ref[...]).astype(BF16)
    q = jnp.dot(n_q, wq_ref[...], preferred_element_type=F32)
    q_ref[...] = (q * q_scale).astype(BF16)
    v_ref[...] = jnp.dot(n_kv, wv_ref[...], preferred_element_type=F32).astype(BF16)
    kt = lax.dot_general(wkt_ref[...], n_kv, (((1,), (1,)), ((), ())), preferred_element_type=F32)
    kt_ref[...] = kt.astype(BF16)


def _project(x, g_q, g_kv, wq, wkt, wv, *, q_scale, shared_norm):
    b, s, d = x.shape
    tm = TOKEN_BLOCK
    kern = functools.partial(_proj_kernel, q_scale=q_scale, shared_norm=shared_norm)
    wspec = _resident((d, d), lambda i, j: (0, 0))
    gspec = _resident((1, d), lambda i, j: (0, 0))
    return pl.pallas_call(
        kern,
        grid=(b, s // tm),
        in_specs=[pl.BlockSpec((None, tm, d), lambda i, j: (i, j, 0)), gspec, gspec, wspec, wspec, wspec],
        out_specs=[pl.BlockSpec((None, tm, d), lambda i, j: (i, j, 0)),
                   pl.BlockSpec((None, d, tm), lambda i, j: (i, 0, j)),
                   pl.BlockSpec((None, tm, d), lambda i, j: (i, j, 0))],
        out_shape=[jax.ShapeDtypeStruct((b, s, d), BF16),
                   jax.ShapeDtypeStruct((b, d, s), BF16),
                   jax.ShapeDtypeStruct((b, s, d), BF16)],
        compiler_params=_params(40, 2),
        name="proj",
    )(x, g_q, g_kv, wq, wkt, wv)


def _attn_a_kernel(q_ref, kt_ref, v_ref, tb_ref, o_ref):
    tq, win = ATTN_BLOCK, ATTN_A_WINDOW
    q0 = pl.program_id(2) * tq
    ws = jnp.maximum(q0 - LEFT_KEYS, 0)
    coff = pl.multiple_of(LEFT_KEYS - (q0 - ws), tq)
    ws = pl.multiple_of(ws, tq)
    q2 = q_ref[...]
    kt = kt_ref[:, pl.ds(ws, win)]
    v = v_ref[pl.ds(ws, win), :]
    lane = lax.broadcasted_iota(jnp.int32, (tq, V7X_LANES), 1)
    outs = []
    for h in range(HEADS_PER_LANE_TILE):
        in_head = (lane >= h * HEAD_DIM) & (lane < (h + 1) * HEAD_DIM)
        qh = jnp.where(in_head, q2, jnp.zeros_like(q2))
        s = jnp.dot(qh, kt, preferred_element_type=F32) + tb_ref[h, :, pl.ds(coff, win)]
        m = jnp.max(s, axis=-1, keepdims=True)
        p = jnp.exp(s - m)
        l = jnp.sum(p, axis=-1, keepdims=True)
        outs.append(jnp.dot(p.astype(BF16), v, preferred_element_type=F32) / l)
    o_ref[...] = jnp.where(lane < HEAD_DIM, outs[0], outs[1]).astype(BF16)


def _bias_table(rel_bias):
    tq = ATTN_BLOCK
    n_col = LEFT_KEYS + ATTN_A_WINDOW
    period = n_col + tq
    m = np.arange(period)
    m = np.where(m < n_col, m, m - period)
    idx = np.clip(LEFT_KEYS - m, -REL_RIGHT, REL_LEFT) + REL_RIGHT
    w = rel_bias.astype(F32)[:, idx]
    skew = jnp.tile(w, (1, tq))[:, :tq * (period - 1)].reshape(-1, tq, period - 1)[:, :, :n_col]
    qi = np.arange(tq)[:, None]
    dk = np.arange(n_col)[None, :] - LEFT_KEYS
    chunk_gap = qi // CHUNK - np.floor_divide(dk, CHUNK)
    in_band = (chunk_gap >= 0) & (chunk_gap <= LEFT_CHUNKS)
    return jnp.where(in_band[None], skew, NEG_INF)


def _attention_a(q, kt, v, table):
    b, s, d = q.shape
    tq = ATTN_BLOCK
    hp = HEADS_PER_LANE_TILE
    return pl.pallas_call(
        _attn_a_kernel,
        grid=(b, N_HEAD_PAIRS, s // tq),
        in_specs=[pl.BlockSpec((None, tq, V7X_LANES), lambda i, p, j: (i, j, p)),
                  pl.BlockSpec((None, V7X_LANES, s), lambda i, p, j: (i, p, 0)),
                  pl.BlockSpec((None, s, V7X_LANES), lambda i, p, j: (i, 0, p)),
                  pl.BlockSpec((hp, tq, table.shape[-1]), lambda i, p, j: (p, 0, 0))],
        out_specs=pl.BlockSpec((None, tq, V7X_LANES), lambda i, p, j: (i, j, p)),
        out_shape=jax.ShapeDtypeStruct((b, s, d), BF16),
        compiler_params=_params(40, 3),
        name="attn_a",
    )(q, kt, v, table)


PIPE_UNROLL = 4
SOFTPLUS2_CLAMP = 126.0


def _softplus2(z):
    return jnp.maximum(jnp.log2(1.0 + jnp.exp2(jnp.minimum(z, SOFTPLUS2_CLAMP))), z)


def _attn_b_kernel(q_ref, kt_ref, v_ref, ntri_ref, o_ref,
                   qh_ref, acc_ref, run_ref, z_ref, d_ref, sp_ref, k0_ref, lat_ref, a_ref):
    t = ATTN_BLOCK
    heads = range(HEADS_PER_LANE_TILE)
    n_q = q_ref.shape[0] // t
    spare = n_q
    lane = lax.broadcasted_iota(jnp.int32, (t, V7X_LANES), 1)
    head_lanes = [(lane >= h * HEAD_DIM) & (lane < (h + 1) * HEAD_DIM) for h in heads]
    causal = (lax.broadcasted_iota(jnp.int32, (t, t), 1) < lax.broadcasted_iota(jnp.int32, (t, t), 0))

    def rows(idx):
        return pl.ds(pl.multiple_of(idx * t, t), t)

    def mask_q(j, carry):
        q2 = q_ref[rows(j), :]
        for h in heads:
            qh_ref[h, rows(j), :] = jnp.where(head_lanes[h], q2, jnp.zeros_like(q2))
        return carry

    lax.fori_loop(0, n_q, mask_q, 0)

    def stage_a(item, slot):
        j, k = item
        kt = kt_ref[:, rows(k)]
        for h in heads:
            qh = qh_ref[h, rows(jnp.minimum(j, n_q - 1)), :]
            z_ref[slot, h] = jnp.dot(qh, kt, preferred_element_type=F32)

    def stage_b(slot, diagonal):
        for h in heads:
            z = z_ref[slot, h]
            sp = _softplus2(z)
            d = z - sp
            if diagonal:
                sp = jnp.where(causal, sp, 0.0)
                d = jnp.where(causal, d, NEG_INF)
            d_ref[slot, h] = d
            sp_ref[slot, h] = sp.astype(BF16)
            k0_ref[slot, h] = sp[:, 0:1] if diagonal else z[:, 0:1]

    def stage_c(slot):
        ntri = ntri_ref[...]
        for h in heads:
            lat_ref[slot, h] = jnp.dot(sp_ref[slot, h], ntri, preferred_element_type=F32)

    def stage_d(item, slot, diagonal):
        j, _ = item
        for h in heads:
            later = lat_ref[slot, h]
            if diagonal:
                a_ref[slot, h] = jnp.exp2(d_ref[slot, h] + later).astype(BF16)
                run_ref[j, h] = later[:, 0:1] - k0_ref[slot, h]
            else:
                log2_a = d_ref[slot, h] + later + run_ref[j, h]
                a_ref[slot, h] = jnp.exp2(log2_a).astype(BF16)
                run_ref[j, h] = log2_a[:, 0:1] - k0_ref[slot, h]

    def stage_e(item, slot, diagonal):
        j, k = item
        v = v_ref[rows(k), :]
        for h in heads:
            pv = jnp.dot(a_ref[slot, h], v, preferred_element_type=F32)
            acc_ref[j, h] = pv if diagonal else acc_ref[j, h] + pv

    def run_pass(first_item, next_item, n_items, diagonal):
        idle = (jnp.int32(spare), jnp.int32(0))

        def step(i, slot, carry):
            gen, in_b, in_c, in_d, in_e = carry
            new = tuple(jnp.where(i < n_items, g, s) for g, s in zip(gen, idle))
            stage_e(in_e, slot, diagonal)
            stage_d(in_d, 1 - slot, diagonal)
            stage_c(slot)
            stage_b(1 - slot, diagonal)
            stage_a(new, slot)
            return next_item(gen), new, in_b, in_c, in_d

        def body(trip, carry):
            for u in range(PIPE_UNROLL):
                carry = step(PIPE_UNROLL * trip + u, u % 2, carry)
            return carry

        lax.fori_loop(0, pl.cdiv(n_items + 4, PIPE_UNROLL), body, (first_item, idle, idle, idle, idle))

    run_pass((jnp.int32(0), jnp.int32(0)), lambda it: (it[0] + 1, it[1] + 1), n_q, True)

    def next_off_diagonal(it):
        j, k = it
        wrap = k == 0
        return jnp.where(wrap, j + 1, j), jnp.where(wrap, j, k - 1)

    run_pass((jnp.int32(1), jnp.int32(0)), next_off_diagonal, n_q * (n_q - 1) // 2, False)

    def write_out(j, carry):
        o_ref[rows(j), :] = jnp.where(lane < HEAD_DIM, acc_ref[j, 0], acc_ref[j, 1]).astype(BF16)
        return carry

    lax.fori_loop(0, n_q, write_out, 0)


def _attention_b(q, kt, v):
    b, s, d = q.shape
    t = ATTN_BLOCK
    hp = HEADS_PER_LANE_TILE
    n_state = s // t + 1
    row = jnp.arange(t, dtype=jnp.int32)[:, None]
    col = jnp.arange(t, dtype=jnp.int32)[None, :]
    ntri = jnp.where(row > col, -1.0, 0.0).astype(BF16)
    return pl.pallas_call(
        _attn_b_kernel,
        grid=(b, N_HEAD_PAIRS),
        in_specs=[pl.BlockSpec((None, s, V7X_LANES), lambda i, p: (i, 0, p)),
                  pl.BlockSpec((None, V7X_LANES, s), lambda i, p: (i, p, 0)),
                  pl.BlockSpec((None, s, V7X_LANES), lambda i, p: (i, 0, p)),
                  _resident((t, t), lambda i, p: (0, 0))],
        out_specs=pl.BlockSpec((None, s, V7X_LANES), lambda i, p: (i, 0, p)),
        out_shape=jax.ShapeDtypeStruct((b, s, d), BF16),
        scratch_shapes=[pltpu.VMEM((hp, s, V7X_LANES), BF16),
                        pltpu.VMEM((n_state, hp, t, V7X_LANES), F32),
                        pltpu.VMEM((n_state, hp, t, 1), F32),
                        pltpu.VMEM((2, hp, t, t), F32),
                        pltpu.VMEM((2, hp, t, t), F32),
                        pltpu.VMEM((2, hp, t, t), BF16),
                        pltpu.VMEM((2, hp, t, 1), F32),
                        pltpu.VMEM((2, hp, t, t), F32),
                        pltpu.VMEM((2, hp, t, t), BF16)],
        compiler_params=_params(48, 2),
        name="attn_b",
    )(q, kt, v, ntri)


def _mlp_kernel(h_ref, a_ref, wo_ref, g_ref, wup_ref, wdn_ref, gf_ref, o_ref, *, final_norm):
    h1 = h_ref[...] + jnp.dot(a_ref[...], wo_ref[...], preferred_element_type=F32)
    n = (h1 * _inv_rms(h1) * g_ref[...]).astype(BF16)
    acc = h1
    for c in range(D_FF // FF_CHUNK):
        u = jnp.dot(n, wup_ref[:, c * FF_CHUNK:(c + 1) * FF_CHUNK], preferred_element_type=F32)
        r = jnp.maximum(u, 0.0)
        acc = acc + jnp.dot((r * r).astype(BF16), wdn_ref[c * FF_CHUNK:(c + 1) * FF_CHUNK, :],
                            preferred_element_type=F32)
    if final_norm:
        acc = acc * _inv_rms(acc) * gf_ref[...]
    o_ref[...] = acc


def _mlp(h, attn, wo, g, wup, wdn, g_final, *, final_norm):
    n_tok, d = h.shape
    tm = TOKEN_BLOCK
    kern = functools.partial(_mlp_kernel, final_norm=final_norm)
    gspec = _resident((1, d), lambda i: (0, 0))
    return pl.pallas_call(
        kern,
        grid=(n_tok // tm,),
        in_specs=[pl.BlockSpec((tm, d), lambda i: (i, 0)),
                  pl.BlockSpec((tm, d), lambda i: (i, 0)),
                  _resident((d, d), lambda i: (0, 0)),
                  gspec,
                  _resident((d, D_FF), lambda i: (0, 0)),
                  _resident((D_FF, d), lambda i: (0, 0)),
                  gspec],
        out_specs=pl.BlockSpec((tm, d), lambda i: (i, 0)),
        out_shape=jax.ShapeDtypeStruct((n_tok, d), F32),
        compiler_params=_params(56, 1),
        name="mlp_final" if final_norm else "mlp",
    )(h, attn, wo, g, wup, wdn, g_final)


def kernel(x, a_norm_attn, a_w_qkv, a_rel_bias, a_w_o, a_norm_mlp, a_w_up, a_w_down, kv_norm, w_kv,
           b_norm_attn, b_w_q, b_w_o, b_norm_mlp, b_w_up, b_w_down, final_norm):
    b, s, d = x.shape
    assert (s % TOKEN_BLOCK, s % ATTN_BLOCK, d) == (0, 0, D_MODEL) and s >= ATTN_A_WINDOW
    assert a_norm_attn.shape[0] == 1 and b_norm_attn.shape[0] == 1
    scale = 1.0 / math.sqrt(HEAD_DIM)
    row = lambda g: g.reshape(1, d).astype(F32)

    wq, wk, wv = (a_w_qkv[0][:, i * d:(i + 1) * d] for i in range(3))
    g_a = row(a_norm_attn[0])
    q, kt, v = _project(x, g_a, g_a, wq.astype(BF16), wk.T.astype(BF16), wv.astype(BF16),
                        q_scale=scale, shared_norm=True)
    attn = _attention_a(q, kt, v, _bias_table(a_rel_bias[0]))
    h = _mlp(x.reshape(b * s, d), attn.reshape(b * s, d), a_w_o[0].astype(BF16), row(a_norm_mlp[0]),
             a_w_up[0].astype(BF16), a_w_down[0].astype(BF16), row(final_norm), final_norm=False)

    q, kt, v = _project(h.reshape(b, s, d), row(b_norm_attn[0]), row(kv_norm),
                        b_w_q[0].astype(BF16), w_kv[:, :d].T.astype(BF16), w_kv[:, d:].astype(BF16),
                        q_scale=scale * math.log2(math.e), shared_norm=False)
    attn = _attention_b(q, kt, v)
    out = _mlp(h, attn.reshape(b * s, d), b_w_o[0].astype(BF16), row(b_norm_mlp[0]),
               b_w_up[0].astype(BF16), b_w_down[0].astype(BF16), row(final_norm), final_norm=True)
    return out.reshape(b, s, d)
```

```python
import functools
import math

import jax
import jax.numpy as jnp
import numpy as np
from jax import lax
from jax.experimental import pallas as pl
from jax.experimental.pallas import tpu as pltpu

D_MODEL = 1024
N_HEADS = 16
HEAD_DIM = D_MODEL // N_HEADS
D_FF = 4 * D_MODEL
CHUNK = 64
LEFT_CHUNKS = 8
LEFT_KEYS = LEFT_CHUNKS * CHUNK
REL_LEFT = 128
REL_RIGHT = CHUNK - 1
EPS = 1e-6
NEG_INF = -1e30
LOG2_E = math.log2(math.e)

V7X_LANES = 128
V7X_MXU_DIM = 256
HEADS_PER_LANE_TILE = V7X_LANES // HEAD_DIM
N_HEAD_PAIRS = N_HEADS // HEADS_PER_LANE_TILE

TOKEN_BLOCK = 512
ATTN_BLOCK = V7X_MXU_DIM
ATTN_A_WINDOW = LEFT_KEYS + ATTN_BLOCK
FF_CHUNK = 1024

F32 = jnp.float32
BF16 = jnp.bfloat16
MIB = 1024 * 1024


def _params(vmem_mib, n_grid):
    return pltpu.CompilerParams(
        dimension_semantics=("arbitrary",) * n_grid,
        vmem_limit_bytes=vmem_mib * MIB,
    )


def _resident(shape, index_map):
    return pl.BlockSpec(shape, index_map, pipeline_mode=pl.Buffered(1))


def _inv_rms(x):
    return lax.rsqrt(jnp.mean(x * x, axis=-1, keepdims=True) + EPS)


def _proj_kernel(x_ref, gq_ref, gkv_ref, wq_ref, wkt_ref, wv_ref, q_ref, kt_ref, v_ref, *,
                 q_scale, shared_norm):
    x = x_ref[...]
    xn = x * _inv_rms(x)
    n_kv = (xn * gkv_ref[...]).astype(BF16)
    n_q = n_kv if shared_norm else (xn * gq_ref[...]).astype(BF16)
    q = jnp.dot(n_q, wq_ref[...], preferred_element_type=F32)
    q_ref[...] = (q * q_scale).astype(BF16)
    v_ref[...] = jnp.dot(n_kv, wv_ref[...], preferred_element_type=F32).astype(BF16)
    kt = lax.dot_general(wkt_ref[...], n_kv, (((1,), (1,)), ((), ())), preferred_element_type=F32)
    kt_ref[...] = kt.astype(BF16)


def _project(x, g_q, g_kv, wq, wkt, wv, *, q_scale, shared_norm):
    b, s, d = x.shape
    tm = TOKEN_BLOCK
    kern = functools.partial(_proj_kernel, q_scale=q_scale, shared_norm=shared_norm)
    wspec = _resident((d, d), lambda i, j: (0, 0))
    gspec = _resident((1, d), lambda i, j: (0, 0))
    return pl.pallas_call(
        kern,
        grid=(b, s // tm),
        in_specs=[pl.BlockSpec((None, tm, d), lambda i, j: (i, j, 0)), gspec, gspec, wspec, wspec, wspec],
        out_specs=[pl.BlockSpec((None, tm, d), lambda i, j: (i, j, 0)),
                   pl.BlockSpec((None, d, tm), lambda i, j: (i, 0, j)),
                   pl.BlockSpec((None, tm, d), lambda i, j: (i, j, 0))],
        out_shape=[jax.ShapeDtypeStruct((b, s, d), BF16),
                   jax.ShapeDtypeStruct((b, d, s), BF16),
                   jax.ShapeDtypeStruct((b, s, d), BF16)],
        compiler_params=_params(40, 2),
        name="proj",
    )(x, g_q, g_kv, wq, wkt, wv)


def _attn_a_kernel(q_ref, kt_ref, v_ref, tb_ref, o_ref):
    tq, win = ATTN_BLOCK, ATTN_A_WINDOW
    q0 = pl.program_id(2) * tq
    ws = jnp.maximum(q0 - LEFT_KEYS, 0)
    coff = pl.multiple_of(LEFT_KEYS - (q0 - ws), tq)
    ws = pl.multiple_of(ws, tq)
    q2 = q_ref[...]
    kt = kt_ref[:, pl.ds(ws, win)]
    v = v_ref[pl.ds(ws, win), :]
    lane = lax.broadcasted_iota(jnp.int32, (tq, V7X_LANES), 1)
    outs = []
    for h in range(HEADS_PER_LANE_TILE):
        in_head = (lane >= h * HEAD_DIM) & (lane < (h + 1) * HEAD_DIM)
        qh = jnp.where(in_head, q2, jnp.zeros_like(q2))
        s = jnp.dot(qh, kt, preferred_element_type=F32) + tb_ref[h, :, pl.ds(coff, win)]
        m = jnp.max(s, axis=-1, keepdims=True)
        p = jnp.exp2(s - m)
        l = jnp.sum(p, axis=-1, keepdims=True)
        outs.append(jnp.dot(p.astype(BF16), v, preferred_element_type=F32) / l)
    o_ref[...] = jnp.where(lane < HEAD_DIM, outs[0], outs[1]).astype(BF16)


def _bias_table(rel_bias):
    tq = ATTN_BLOCK
    n_col = LEFT_KEYS + ATTN_A_WINDOW
    period = n_col + tq
    m = np.arange(period)
    m = np.where(m < n_col, m, m - period)
    idx = np.clip(LEFT_KEYS - m, -REL_RIGHT, REL_LEFT) + REL_RIGHT
    w = rel_bias.astype(F32)[:, idx]
    skew = jnp.tile(w, (1, tq))[:, :tq * (period - 1)].reshape(-1, tq, period - 1)[:, :, :n_col]
    qi = np.arange(tq)[:, None]
    dk = np.arange(n_col)[None, :] - LEFT_KEYS
    chunk_gap = qi // CHUNK - np.floor_divide(dk, CHUNK)
    in_band = (chunk_gap >= 0) & (chunk_gap <= LEFT_CHUNKS)
    return jnp.where(in_band[None], skew * LOG2_E, NEG_INF)


def _attention_a(q, kt, v, table):
    b, s, d = q.shape
    tq = ATTN_BLOCK
    hp = HEADS_PER_LANE_TILE
    return pl.pallas_call(
        _attn_a_kernel,
        grid=(b, N_HEAD_PAIRS, s // tq),
        in_specs=[pl.BlockSpec((None, tq, V7X_LANES), lambda i, p, j: (i, j, p)),
                  pl.BlockSpec((None, V7X_LANES, s), lambda i, p, j: (i, p, 0)),
                  pl.BlockSpec((None, s, V7X_LANES), lambda i, p, j: (i, 0, p)),
                  pl.BlockSpec((hp, tq, table.shape[-1]), lambda i, p, j: (p, 0, 0))],
        out_specs=pl.BlockSpec((None, tq, V7X_LANES), lambda i, p, j: (i, j, p)),
        out_shape=jax.ShapeDtypeStruct((b, s, d), BF16),
        compiler_params=_params(40, 3),
        name="attn_a",
    )(q, kt, v, table)


PIPE_UNROLL = 4
EXIT_LOG2 = -160.0
SOFTPLUS2_CLAMP = 126.0


def _softplus2(z):
    return jnp.maximum(jnp.log2(1.0 + jnp.exp2(jnp.minimum(z, SOFTPLUS2_CLAMP))), z)


def _attn_b_kernel(q_ref, kt_ref, v_ref, ntri_ref, o_ref,
                   qh_ref, acc_ref, run_ref, z_ref, d_ref, sp_ref, k0_ref, lat_ref, a_ref, active_ref, n_active_ref):
    t = ATTN_BLOCK
    heads = range(HEADS_PER_LANE_TILE)
    n_q = q_ref.shape[0] // t
    spare = n_q
    lane = lax.broadcasted_iota(jnp.int32, (t, V7X_LANES), 1)
    head_lanes = [(lane >= h * HEAD_DIM) & (lane < (h + 1) * HEAD_DIM) for h in heads]
    causal = (lax.broadcasted_iota(jnp.int32, (t, t), 1) < lax.broadcasted_iota(jnp.int32, (t, t), 0))

    def rows(idx):
        return pl.ds(pl.multiple_of(idx * t, t), t)

    def mask_q(j, carry):
        q2 = q_ref[rows(j), :]
        for h in heads:
            qh_ref[h, rows(j), :] = jnp.where(head_lanes[h], q2, jnp.zeros_like(q2))
        return carry

    lax.fori_loop(0, n_q, mask_q, 0)

    def stage_a(item, slot):
        j, k = item
        kt = kt_ref[:, rows(k)]
        for h in heads:
            qh = qh_ref[h, rows(jnp.minimum(j, n_q - 1)), :]
            z_ref[slot, h] = jnp.dot(qh, kt, preferred_element_type=F32)

    def stage_b(slot, diagonal):
        for h in heads:
            z = z_ref[slot, h]
            sp = _softplus2(z)
            d = z - sp
            if diagonal:
                sp = jnp.where(causal, sp, 0.0)
                d = jnp.where(causal, d, NEG_INF)
            d_ref[slot, h] = d
            sp_ref[slot, h] = sp.astype(BF16)
            k0_ref[slot, h] = sp[:, 0:1] if diagonal else z[:, 0:1]

    def stage_c(slot):
        ntri = ntri_ref[...]
        for h in heads:
            lat_ref[slot, h] = jnp.dot(sp_ref[slot, h], ntri, preferred_element_type=F32)

    def stage_d(item, slot, diagonal):
        j, _ = item
        tile_max = None
        for h in heads:
            later = lat_ref[slot, h]
            if diagonal:
                a_ref[slot, h] = jnp.exp2(d_ref[slot, h] + later).astype(BF16)
                run_ref[j, h] = later[:, 0:1] - k0_ref[slot, h]
            else:
                log2_a = d_ref[slot, h] + later + run_ref[j, h]
                a_ref[slot, h] = jnp.exp2(log2_a).astype(BF16)
                run_new = log2_a[:, 0:1] - k0_ref[slot, h]
                run_ref[j, h] = run_new
                head_max = jnp.max(run_new)
                tile_max = head_max if tile_max is None else jnp.maximum(tile_max, head_max)
        if not diagonal:
            finished = tile_max <= EXIT_LOG2
            was_active = active_ref[j]
            active_ref[j] = jnp.where(finished, 0, was_active)
            n_active_ref[0] = n_active_ref[0] - jnp.where(finished, was_active, 0)

    def stage_e(item, slot, diagonal):
        j, k = item
        v = v_ref[rows(k), :]
        for h in heads:
            pv = jnp.dot(a_ref[slot, h], v, preferred_element_type=F32)
            acc_ref[j, h] = pv if diagonal else acc_ref[j, h] + pv

    def pipeline_step(slot, new, in_b, in_c, in_d, in_e, diagonal):
        stage_e(in_e, slot, diagonal)
        stage_d(in_d, 1 - slot, diagonal)
        stage_c(slot)
        stage_b(1 - slot, diagonal)
        stage_a(new, slot)

    idle = (jnp.int32(spare), jnp.int32(0))
    n_stages = 5

    def diagonal_trip(trip, carry):
        for u in range(PIPE_UNROLL):
            p, in_b, in_c, in_d, in_e = carry
            real = p < n_q
            new = (jnp.where(real, p, spare), jnp.where(real, p, 0))
            pipeline_step(u % 2, new, in_b, in_c, in_d, in_e, True)
            carry = (p + 1, new, in_b, in_c, in_d)
        return carry

    lax.fori_loop(0, pl.cdiv(n_q + n_stages - 1, PIPE_UNROLL), diagonal_trip,
                  (jnp.int32(0), idle, idle, idle, idle))

    def mark_active(j, carry):
        active_ref[j] = jnp.where((j >= 1) & (j < n_q), 1, 0)
        return carry

    lax.fori_loop(0, n_q + 1, mark_active, 0)
    n_active_ref[0] = n_q - 1

    def off_diagonal_trip(carry):
        any_active = n_active_ref[0] > 0
        for u in range(PIPE_UNROLL):
            (j, gap), in_b, in_c, in_d, in_e, cooldown = carry
            real = any_active & (gap < n_q)
            new = (jnp.where(real, j, spare), jnp.where(real, j - gap, 0))
            pipeline_step(u % 2, new, in_b, in_c, in_d, in_e, False)
            last_of_gap = j == n_q - 1
            gen = (jnp.where(real, jnp.where(last_of_gap, gap + 1, j + 1), j),
                   jnp.where(real & last_of_gap, gap + 1, gap))
            cooldown = jnp.where(real, n_stages - 1, jnp.maximum(cooldown - 1, 0))
            carry = (gen, new, in_b, in_c, in_d, cooldown)
        return carry

    lax.while_loop(lambda carry: carry[-1] > 0, off_diagonal_trip,
                   ((jnp.int32(1), jnp.int32(1)), idle, idle, idle, idle, jnp.int32(1)))

    def write_out(j, carry):
        o_ref[rows(j), :] = jnp.where(lane < HEAD_DIM, acc_ref[j, 0], acc_ref[j, 1]).astype(BF16)
        return carry

    lax.fori_loop(0, n_q, write_out, 0)


def _attention_b(q, kt, v):
    b, s, d = q.shape
    t = ATTN_BLOCK
    hp = HEADS_PER_LANE_TILE
    n_state = s // t + 1
    row = jnp.arange(t, dtype=jnp.int32)[:, None]
    col = jnp.arange(t, dtype=jnp.int32)[None, :]
    ntri = jnp.where(row > col, -1.0, 0.0).astype(BF16)
    return pl.pallas_call(
        _attn_b_kernel,
        grid=(b, N_HEAD_PAIRS),
        in_specs=[pl.BlockSpec((None, s, V7X_LANES), lambda i, p: (i, 0, p)),
                  pl.BlockSpec((None, V7X_LANES, s), lambda i, p: (i, p, 0)),
                  pl.BlockSpec((None, s, V7X_LANES), lambda i, p: (i, 0, p)),
                  _resident((t, t), lambda i, p: (0, 0))],
        out_specs=pl.BlockSpec((None, s, V7X_LANES), lambda i, p: (i, 0, p)),
        out_shape=jax.ShapeDtypeStruct((b, s, d), BF16),
        scratch_shapes=[pltpu.VMEM((hp, s, V7X_LANES), BF16),
                        pltpu.VMEM((n_state, hp, t, V7X_LANES), F32),
                        pltpu.VMEM((n_state, hp, t, 1), F32),
                        pltpu.VMEM((2, hp, t, t), F32),
                        pltpu.VMEM((2, hp, t, t), F32),
                        pltpu.VMEM((2, hp, t, t), BF16),
                        pltpu.VMEM((2, hp, t, 1), F32),
                        pltpu.VMEM((2, hp, t, t), F32),
                        pltpu.VMEM((2, hp, t, t), BF16),
                        pltpu.SMEM((n_state,), jnp.int32),
                        pltpu.SMEM((1,), jnp.int32)],
        compiler_params=_params(48, 2),
        name="attn_b",
    )(q, kt, v, ntri)


def _mlp_kernel(h_ref, a_ref, wo_ref, g_ref, wup_ref, wdn_ref, gf_ref, o_ref, *, final_norm):
    h1 = h_ref[...] + jnp.dot(a_ref[...], wo_ref[...], preferred_element_type=F32)
    n = (h1 * _inv_rms(h1) * g_ref[...]).astype(BF16)
    acc = h1
    for c in range(D_FF // FF_CHUNK):
        u = jnp.dot(n, wup_ref[:, c * FF_CHUNK:(c + 1) * FF_CHUNK], preferred_element_type=F32)
        r = jnp.maximum(u, 0.0)
        acc = acc + jnp.dot((r * r).astype(BF16), wdn_ref[c * FF_CHUNK:(c + 1) * FF_CHUNK, :],
                            preferred_element_type=F32)
    if final_norm:
        acc = acc * _inv_rms(acc) * gf_ref[...]
    o_ref[...] = acc


def _mlp(h, attn, wo, g, wup, wdn, g_final, *, final_norm):
    n_tok, d = h.shape
    tm = TOKEN_BLOCK
    kern = functools.partial(_mlp_kernel, final_norm=final_norm)
    gspec = _resident((1, d), lambda i: (0, 0))
    return pl.pallas_call(
        kern,
        grid=(n_tok // tm,),
        in_specs=[pl.BlockSpec((tm, d), lambda i: (i, 0)),
                  pl.BlockSpec((tm, d), lambda i: (i, 0)),
                  _resident((d, d), lambda i: (0, 0)),
                  gspec,
                  _resident((d, D_FF), lambda i: (0, 0)),
                  _resident((D_FF, d), lambda i: (0, 0)),
                  gspec],
        out_specs=pl.BlockSpec((tm, d), lambda i: (i, 0)),
        out_shape=jax.ShapeDtypeStruct((n_tok, d), F32),
        compiler_params=_params(56, 1),
        name="mlp_final" if final_norm else "mlp",
    )(h, attn, wo, g, wup, wdn, g_final)


def kernel(x, a_norm_attn, a_w_qkv, a_rel_bias, a_w_o, a_norm_mlp, a_w_up, a_w_down, kv_norm, w_kv,
           b_norm_attn, b_w_q, b_w_o, b_norm_mlp, b_w_up, b_w_down, final_norm):
    b, s, d = x.shape
    assert (s % TOKEN_BLOCK, s % ATTN_BLOCK, d) == (0, 0, D_MODEL) and s >= ATTN_A_WINDOW
    assert a_norm_attn.shape[0] == 1 and b_norm_attn.shape[0] == 1
    scale = 1.0 / math.sqrt(HEAD_DIM)
    row = lambda g: g.reshape(1, d).astype(F32)

    wq, wk, wv = (a_w_qkv[0][:, i * d:(i + 1) * d] for i in range(3))
    g_a = row(a_norm_attn[0])
    q, kt, v = _project(x, g_a, g_a, wq.astype(BF16), wk.T.astype(BF16), wv.astype(BF16),
                        q_scale=scale * LOG2_E, shared_norm=True)
    attn = _attention_a(q, kt, v, _bias_table(a_rel_bias[0]))
    h = _mlp(x.reshape(b * s, d), attn.reshape(b * s, d), a_w_o[0].astype(BF16), row(a_norm_mlp[0]),
             a_w_up[0].astype(BF16), a_w_down[0].astype(BF16), row(final_norm), final_norm=False)

    q, kt, v = _project(h.reshape(b, s, d), row(b_norm_attn[0]), row(kv_norm),
                        b_w_q[0].astype(BF16), w_kv[:, :d].T.astype(BF16), w_kv[:, d:].astype(BF16),
                        q_scale=scale * LOG2_E, shared_norm=False)
    attn = _attention_b(q, kt, v)
    out = _mlp(h, attn.reshape(b * s, d), b_w_o[0].astype(BF16), row(b_norm_mlp[0]),
               b_w_up[0].astype(BF16), b_w_down[0].astype(BF16), row(final_norm), final_norm=True)
    return out.reshape(b, s, d)
```

```python
import functools
import math

import jax
import jax.numpy as jnp
import numpy as np
from jax import lax
from jax.experimental import pallas as pl
from jax.experimental.pallas import tpu as pltpu

D_MODEL = 1024
N_HEADS = 16
HEAD_DIM = D_MODEL // N_HEADS
D_FF = 4 * D_MODEL
CHUNK = 64
LEFT_CHUNKS = 8
LEFT_KEYS = LEFT_CHUNKS * CHUNK
REL_LEFT = 128
REL_RIGHT = CHUNK - 1
EPS = 1e-6
NEG_INF = -1e30
LOG2_E = math.log2(math.e)

V7X_LANES = 128
V7X_MXU_DIM = 256
HEADS_PER_LANE_TILE = V7X_LANES // HEAD_DIM
N_HEAD_PAIRS = N_HEADS // HEADS_PER_LANE_TILE

TOKEN_BLOCK = 512
ATTN_BLOCK = V7X_MXU_DIM
ATTN_A_WINDOW = LEFT_KEYS + ATTN_BLOCK
PIPE_UNROLL = 4
FF_CHUNK = 1024

F32 = jnp.float32
BF16 = jnp.bfloat16
MIB = 1024 * 1024


def _params(vmem_mib, n_grid):
    return pltpu.CompilerParams(
        dimension_semantics=("arbitrary",) * n_grid,
        vmem_limit_bytes=vmem_mib * MIB,
    )


def _resident(shape, index_map):
    return pl.BlockSpec(shape, index_map, pipeline_mode=pl.Buffered(1))


def _inv_rms(x):
    return lax.rsqrt(jnp.mean(x * x, axis=-1, keepdims=True) + EPS)


def _proj_kernel(x_ref, gq_ref, gkv_ref, wq_ref, wkt_ref, wv_ref, q_ref, kt_ref, v_ref, *,
                 q_scale, shared_norm):
    x = x_ref[...]
    xn = x * _inv_rms(x)
    n_kv = (xn * gkv_ref[...]).astype(BF16)
    n_q = n_kv if shared_norm else (xn * gq_ref[...]).astype(BF16)
    q = (jnp.dot(n_q, wq_ref[...], preferred_element_type=F32) * q_scale).astype(BF16)
    v = jnp.dot(n_kv, wv_ref[...], preferred_element_type=F32).astype(BF16)
    for p in range(N_HEAD_PAIRS):
        q_ref[p] = q[:, p * V7X_LANES:(p + 1) * V7X_LANES]
        v_ref[p] = v[:, p * V7X_LANES:(p + 1) * V7X_LANES]
    kt = lax.dot_general(wkt_ref[...], n_kv, (((1,), (1,)), ((), ())), preferred_element_type=F32)
    kt_ref[...] = kt.astype(BF16)


def _project(x, g_q, g_kv, wq, wkt, wv, *, q_scale, shared_norm):
    b, s, d = x.shape
    tm = TOKEN_BLOCK
    kern = functools.partial(_proj_kernel, q_scale=q_scale, shared_norm=shared_norm)
    wspec = _resident((d, d), lambda i, j: (0, 0))
    gspec = _resident((1, d), lambda i, j: (0, 0))
    return pl.pallas_call(
        kern,
        grid=(b, s // tm),
        in_specs=[pl.BlockSpec((None, tm, d), lambda i, j: (i, j, 0)), gspec, gspec, wspec, wspec, wspec],
        out_specs=[pl.BlockSpec((None, N_HEAD_PAIRS, tm, V7X_LANES), lambda i, j: (i, 0, j, 0)),
                   pl.BlockSpec((None, d, tm), lambda i, j: (i, 0, j)),
                   pl.BlockSpec((None, N_HEAD_PAIRS, tm, V7X_LANES), lambda i, j: (i, 0, j, 0))],
        out_shape=[jax.ShapeDtypeStruct((b, N_HEAD_PAIRS, s, V7X_LANES), BF16),
                   jax.ShapeDtypeStruct((b, d, s), BF16),
                   jax.ShapeDtypeStruct((b, N_HEAD_PAIRS, s, V7X_LANES), BF16)],
        compiler_params=_params(40, 2),
        name="proj",
    )(x, g_q, g_kv, wq, wkt, wv)


def _attn_a_kernel(q_ref, kt_ref, v_ref, tb_ref, o_ref, qh_ref, s_ref, p_ref, l_ref):
    tq, win = ATTN_BLOCK, ATTN_A_WINDOW
    heads = range(HEADS_PER_LANE_TILE)
    n_q = q_ref.shape[0] // tq
    lane = lax.broadcasted_iota(jnp.int32, (tq, V7X_LANES), 1)
    head_lanes = [(lane >= h * HEAD_DIM) & (lane < (h + 1) * HEAD_DIM) for h in heads]

    def rows(idx):
        return pl.ds(pl.multiple_of(idx * tq, tq), tq)

    def window(j):
        q0 = j * tq
        ws = jnp.maximum(q0 - LEFT_KEYS, 0)
        coff = pl.multiple_of(LEFT_KEYS - (q0 - ws), tq)
        return pl.ds(pl.multiple_of(ws, tq), win), pl.ds(coff, win)

    def mask_q(j, carry):
        q2 = q_ref[rows(j), :]
        for h in heads:
            qh_ref[h, rows(j), :] = jnp.where(head_lanes[h], q2, jnp.zeros_like(q2))
        return carry

    lax.fori_loop(0, n_q, mask_q, 0)

    def stage_a(j, slot):
        keys, cols = window(j)
        kt = kt_ref[:, keys]
        for h in heads:
            s_ref[slot, h] = (jnp.dot(qh_ref[h, rows(j), :], kt, preferred_element_type=F32)
                              + tb_ref[h, :, cols])

    def stage_b(slot):
        for h in heads:
            s = s_ref[slot, h]
            p = jnp.exp2(s - jnp.max(s, axis=-1, keepdims=True))
            l_ref[slot, h] = jnp.sum(p, axis=-1, keepdims=True)
            p_ref[slot, h] = p.astype(BF16)

    def stage_c(j, slot):
        keys, _ = window(j)
        v = v_ref[keys, :]
        outs = [jnp.dot(p_ref[slot, h], v, preferred_element_type=F32) / l_ref[slot, h] for h in heads]
        o_ref[rows(j), :] = jnp.where(lane < HEAD_DIM, outs[0], outs[1]).astype(BF16)

    def trip(t, carry):
        for u in range(2):
            i, in_b, in_c = carry
            new = jnp.minimum(i, n_q - 1)
            stage_c(in_c, u)
            stage_b(1 - u)
            stage_a(new, u)
            carry = (i + 1, new, in_b)
        return carry

    zero = jnp.int32(0)
    lax.fori_loop(0, pl.cdiv(n_q + 2, 2), trip, (zero, zero, zero))


def _bias_table(rel_bias):
    tq = ATTN_BLOCK
    n_col = LEFT_KEYS + ATTN_A_WINDOW
    period = n_col + tq
    m = np.arange(period)
    m = np.where(m < n_col, m, m - period)
    idx = np.clip(LEFT_KEYS - m, -REL_RIGHT, REL_LEFT) + REL_RIGHT
    w = rel_bias.astype(F32)[:, idx]
    skew = jnp.tile(w, (1, tq))[:, :tq * (period - 1)].reshape(-1, tq, period - 1)[:, :, :n_col]
    qi = np.arange(tq)[:, None]
    dk = np.arange(n_col)[None, :] - LEFT_KEYS
    chunk_gap = qi // CHUNK - np.floor_divide(dk, CHUNK)
    in_band = (chunk_gap >= 0) & (chunk_gap <= LEFT_CHUNKS)
    return jnp.where(in_band[None], skew * LOG2_E, NEG_INF)


def _attention_a(q, kt, v, table):
    b, _, s, _ = q.shape
    tq = ATTN_BLOCK
    hp = HEADS_PER_LANE_TILE
    return pl.pallas_call(
        _attn_a_kernel,
        grid=(b, N_HEAD_PAIRS),
        in_specs=[pl.BlockSpec((None, None, s, V7X_LANES), lambda i, p: (i, p, 0, 0)),
                  pl.BlockSpec((None, V7X_LANES, s), lambda i, p: (i, p, 0)),
                  pl.BlockSpec((None, None, s, V7X_LANES), lambda i, p: (i, p, 0, 0)),
                  pl.BlockSpec((hp, tq, table.shape[-1]), lambda i, p: (p, 0, 0))],
        out_specs=pl.BlockSpec((None, None, s, V7X_LANES), lambda i, p: (i, p, 0, 0)),
        out_shape=jax.ShapeDtypeStruct(q.shape, BF16),
        scratch_shapes=[pltpu.VMEM((hp, s, V7X_LANES), BF16),
                        pltpu.VMEM((2, hp, tq, ATTN_A_WINDOW), F32),
                        pltpu.VMEM((2, hp, tq, ATTN_A_WINDOW), BF16),
                        pltpu.VMEM((2, hp, tq, 1), F32)],
        compiler_params=_params(48, 2),
        name="attn_a",
    )(q, kt, v, table)


EXIT_LOG2 = -160.0
SOFTPLUS2_CLAMP = 126.0


def _softplus2(z):
    return jnp.log2(1.0 + jnp.exp2(z))


def _attn_b_kernel(q_ref, kt_ref, v_ref, ntri_ref, o_ref,
                   qh_ref, acc_ref, run_ref, z_ref, d_ref, sp_ref, k0_ref, lat_ref, a_ref, active_ref, n_active_ref):
    t = ATTN_BLOCK
    heads = range(HEADS_PER_LANE_TILE)
    n_q = q_ref.shape[0] // t
    spare = n_q
    lane = lax.broadcasted_iota(jnp.int32, (t, V7X_LANES), 1)
    head_lanes = [(lane >= h * HEAD_DIM) & (lane < (h + 1) * HEAD_DIM) for h in heads]
    causal = (lax.broadcasted_iota(jnp.int32, (t, t), 1) < lax.broadcasted_iota(jnp.int32, (t, t), 0))

    def rows(idx):
        return pl.ds(pl.multiple_of(idx * t, t), t)

    def mask_q(j, carry):
        q2 = q_ref[rows(j), :]
        for h in heads:
            qh_ref[h, rows(j), :] = jnp.where(head_lanes[h], q2, jnp.zeros_like(q2))
        return carry

    lax.fori_loop(0, n_q, mask_q, 0)

    def stage_a(item, slot):
        j, k = item
        kt = kt_ref[:, rows(k)]
        for h in heads:
            qh = qh_ref[h, rows(jnp.minimum(j, n_q - 1)), :]
            z_ref[slot, h] = jnp.dot(qh, kt, preferred_element_type=F32)

    def stage_b(slot, diagonal):
        for h in heads:
            z = jnp.minimum(z_ref[slot, h], SOFTPLUS2_CLAMP)
            sp = _softplus2(z)
            d = z - sp
            if diagonal:
                sp = jnp.where(causal, sp, 0.0)
                d = jnp.where(causal, d, NEG_INF)
            d_ref[slot, h] = d
            sp_ref[slot, h] = sp.astype(BF16)
            k0_ref[slot, h] = sp[:, 0:1] if diagonal else z[:, 0:1]

    def stage_c(slot):
        ntri = ntri_ref[...]
        for h in heads:
            lat_ref[slot, h] = jnp.dot(sp_ref[slot, h], ntri, preferred_element_type=F32)

    def stage_d(item, slot, diagonal):
        j, _ = item
        tile_max = None
        for h in heads:
            later = lat_ref[slot, h]
            if diagonal:
                a_ref[slot, h] = jnp.exp2(d_ref[slot, h] + later).astype(BF16)
                run_ref[j, h] = later[:, 0:1] - k0_ref[slot, h]
            else:
                log2_a = d_ref[slot, h] + later + run_ref[j, h]
                a_ref[slot, h] = jnp.exp2(log2_a).astype(BF16)
                run_new = log2_a[:, 0:1] - k0_ref[slot, h]
                run_ref[j, h] = run_new
                head_max = jnp.max(run_new)
                tile_max = head_max if tile_max is None else jnp.maximum(tile_max, head_max)
        if not diagonal:
            finished = tile_max <= EXIT_LOG2
            was_active = active_ref[j]
            active_ref[j] = jnp.where(finished, 0, was_active)
            n_active_ref[0] = n_active_ref[0] - jnp.where(finished, was_active, 0)

    def stage_e(item, slot, diagonal):
        j, k = item
        v = v_ref[rows(k), :]
        for h in heads:
            pv = jnp.dot(a_ref[slot, h], v, preferred_element_type=F32)
            acc_ref[j, h] = pv if diagonal else acc_ref[j, h] + pv

    def pipeline_step(slot, new, in_b, in_c, in_d, in_e, diagonal):
        diag_b, diag_d, diag_e = diagonal
        stage_e(in_e, slot, diag_e)
        stage_d(in_d, 1 - slot, diag_d)
        stage_c(slot)
        stage_b(1 - slot, diag_b)
        stage_a(new, slot)

    idle = (jnp.int32(spare), jnp.int32(0))
    n_stages = 5

    def mark_active(j, carry):
        active_ref[j] = jnp.where((j >= 1) & (j < n_q), 1, 0)
        return carry

    lax.fori_loop(0, n_q + 1, mark_active, 0)
    n_active_ref[0] = n_q - 1

    def diagonal_trip(trip, carry):
        for u in range(PIPE_UNROLL):
            p, in_b, in_c, in_d, in_e = carry
            real = p < n_q
            new = (jnp.where(real, p, spare), jnp.where(real, p, 0))
            pipeline_step(u % 2, new, in_b, in_c, in_d, in_e, (True, True, True))
            carry = (p + 1, new, in_b, in_c, in_d)
        return carry

    _, in_b, in_c, in_d, in_e = lax.fori_loop(0, pl.cdiv(n_q, PIPE_UNROLL), diagonal_trip,
                                              (jnp.int32(0), idle, idle, idle, idle))

    first_items = [(j, j - gap) for gap in range(1, n_q) for j in range(gap, n_q)][:PIPE_UNROLL + 1]
    for u in range(PIPE_UNROLL):
        new = (jnp.int32(first_items[u][0]), jnp.int32(first_items[u][1])) if u < len(first_items) else idle
        pipeline_step(u % 2, new, in_b, in_c, in_d, in_e, (u < 1, u < 3, u < 4))
        in_b, in_c, in_d, in_e = new, in_b, in_c, in_d
    if len(first_items) > PIPE_UNROLL:
        j_next, k_next = first_items[PIPE_UNROLL]
        cursor = (jnp.int32(j_next), jnp.int32(j_next - k_next))
    else:
        cursor = (jnp.int32(n_q - 1), jnp.int32(n_q))
    exhausted = (jnp.int32(n_q - 1), jnp.int32(n_q))

    def advance(cur):
        j, gap = cur
        wrap = j >= n_q - 1
        return jnp.where(wrap, gap + 1, j + 1), jnp.where(wrap, gap + 1, gap)

    def next_active(cur):
        return lax.while_loop(lambda c: (c[1] < n_q) & (active_ref[jnp.minimum(c[0], spare)] == 0), advance, cur)

    def off_diagonal_trip(carry):
        cur, in_b, in_c, in_d, in_e, cooldown = carry
        none_active = n_active_ref[0] == 0
        cur = tuple(jnp.where(none_active, e, c) for e, c in zip(exhausted, cur))
        items = []
        for u in range(PIPE_UNROLL):
            cur = next_active(cur)
            items.append(cur)
            cur = advance(cur)
        for u in range(PIPE_UNROLL):
            j, gap = items[u]
            real = gap < n_q
            new = (jnp.where(real, j, spare), jnp.where(real, j - gap, 0))
            pipeline_step(u % 2, new, in_b, in_c, in_d, in_e, (False, False, False))
            cooldown = jnp.where(real, n_stages - 1, jnp.maximum(cooldown - 1, 0))
            in_b, in_c, in_d, in_e = new, in_b, in_c, in_d
        return cur, in_b, in_c, in_d, in_e, cooldown

    lax.while_loop(lambda carry: carry[-1] > 0, off_diagonal_trip,
                   (cursor, in_b, in_c, in_d, in_e, jnp.int32(n_stages - 1)))

    def write_out(j, carry):
        o_ref[rows(j), :] = jnp.where(lane < HEAD_DIM, acc_ref[j, 0], acc_ref[j, 1]).astype(BF16)
        return carry

    lax.fori_loop(0, n_q, write_out, 0)


def _attention_b(q, kt, v):
    b, _, s, _ = q.shape
    t = ATTN_BLOCK
    hp = HEADS_PER_LANE_TILE
    n_state = s // t + 1
    row = jnp.arange(t, dtype=jnp.int32)[:, None]
    col = jnp.arange(t, dtype=jnp.int32)[None, :]
    ntri = jnp.where(row > col, -1.0, 0.0).astype(BF16)
    return pl.pallas_call(
        _attn_b_kernel,
        grid=(b, N_HEAD_PAIRS),
        in_specs=[pl.BlockSpec((None, None, s, V7X_LANES), lambda i, p: (i, p, 0, 0)),
                  pl.BlockSpec((None, V7X_LANES, s), lambda i, p: (i, p, 0)),
                  pl.BlockSpec((None, None, s, V7X_LANES), lambda i, p: (i, p, 0, 0)),
                  _resident((t, t), lambda i, p: (0, 0))],
        out_specs=pl.BlockSpec((None, None, s, V7X_LANES), lambda i, p: (i, p, 0, 0)),
        out_shape=jax.ShapeDtypeStruct(q.shape, BF16),
        scratch_shapes=[pltpu.VMEM((hp, s, V7X_LANES), BF16),
                        pltpu.VMEM((n_state, hp, t, V7X_LANES), F32),
                        pltpu.VMEM((n_state, hp, t, 1), F32),
                        pltpu.VMEM((2, hp, t, t), F32),
                        pltpu.VMEM((2, hp, t, t), F32),
                        pltpu.VMEM((2, hp, t, t), BF16),
                        pltpu.VMEM((2, hp, t, 1), F32),
                        pltpu.VMEM((2, hp, t, t), F32),
                        pltpu.VMEM((2, hp, t, t), BF16),
                        pltpu.SMEM((n_state,), jnp.int32),
                        pltpu.SMEM((1,), jnp.int32)],
        compiler_params=_params(48, 2),
        name="attn_b",
    )(q, kt, v, ntri)


def _mlp_kernel(h_ref, a_ref, wo_ref, g_ref, wup_ref, wdn_ref, gf_ref, o_ref, *, final_norm):
    attn = jnp.concatenate([a_ref[p] for p in range(N_HEAD_PAIRS)], axis=-1)
    h1 = h_ref[...] + jnp.dot(attn, wo_ref[...], preferred_element_type=F32)
    n = (h1 * _inv_rms(h1) * g_ref[...]).astype(BF16)
    acc = h1
    for c in range(D_FF // FF_CHUNK):
        u = jnp.dot(n, wup_ref[:, c * FF_CHUNK:(c + 1) * FF_CHUNK], preferred_element_type=F32)
        r = jnp.maximum(u, 0.0)
        acc = acc + jnp.dot((r * r).astype(BF16), wdn_ref[c * FF_CHUNK:(c + 1) * FF_CHUNK, :],
                            preferred_element_type=F32)
    if final_norm:
        acc = acc * _inv_rms(acc) * gf_ref[...]
    o_ref[...] = acc


def _mlp(h, attn, wo, g, wup, wdn, g_final, *, final_norm):
    n_tok, d = h.shape
    tm = TOKEN_BLOCK
    blocks_per_seq = attn.shape[2] // tm
    kern = functools.partial(_mlp_kernel, final_norm=final_norm)
    gspec = _resident((1, d), lambda i: (0, 0))
    return pl.pallas_call(
        kern,
        grid=(n_tok // tm,),
        in_specs=[pl.BlockSpec((tm, d), lambda i: (i, 0)),
                  pl.BlockSpec((None, N_HEAD_PAIRS, tm, V7X_LANES),
                               lambda i: (i // blocks_per_seq, 0, i % blocks_per_seq, 0)),
                  _resident((d, d), lambda i: (0, 0)),
                  gspec,
                  _resident((d, D_FF), lambda i: (0, 0)),
                  _resident((D_FF, d), lambda i: (0, 0)),
                  gspec],
        out_specs=pl.BlockSpec((tm, d), lambda i: (i, 0)),
        out_shape=jax.ShapeDtypeStruct((n_tok, d), F32),
        compiler_params=_params(56, 1),
        name="mlp_final" if final_norm else "mlp",
    )(h, attn, wo, g, wup, wdn, g_final)


def kernel(x, a_norm_attn, a_w_qkv, a_rel_bias, a_w_o, a_norm_mlp, a_w_up, a_w_down, kv_norm, w_kv,
           b_norm_attn, b_w_q, b_w_o, b_norm_mlp, b_w_up, b_w_down, final_norm):
    b, s, d = x.shape
    assert (s % TOKEN_BLOCK, s % ATTN_BLOCK, d) == (0, 0, D_MODEL) and s >= ATTN_A_WINDOW
    assert a_norm_attn.shape[0] == 1 and b_norm_attn.shape[0] == 1
    scale = 1.0 / math.sqrt(HEAD_DIM)
    row = lambda g: g.reshape(1, d).astype(F32)

    wq, wk, wv = (a_w_qkv[0][:, i * d:(i + 1) * d] for i in range(3))
    g_a = row(a_norm_attn[0])
    q, kt, v = _project(x, g_a, g_a, wq.astype(BF16), wk.T.astype(BF16), wv.astype(BF16),
                        q_scale=scale * LOG2_E, shared_norm=True)
    attn = _attention_a(q, kt, v, _bias_table(a_rel_bias[0]))
    h = _mlp(x.reshape(b * s, d), attn, a_w_o[0].astype(BF16), row(a_norm_mlp[0]),
             a_w_up[0].astype(BF16), a_w_down[0].astype(BF16), row(final_norm), final_norm=False)

    q, kt, v = _project(h.reshape(b, s, d), row(b_norm_attn[0]), row(kv_norm),
                        b_w_q[0].astype(BF16), w_kv[:, :d].T.astype(BF16), w_kv[:, d:].astype(BF16),
                        q_scale=scale * LOG2_E, shared_norm=False)
    attn = _attention_b(q, kt, v)
    out = _mlp(h, attn, b_w_o[0].astype(BF16), row(b_norm_mlp[0]),
               b_w_up[0].astype(BF16), b_w_down[0].astype(BF16), row(final_norm), final_norm=True)
    return out.reshape(b, s, d)
```

```python
import functools
import math

import jax
import jax.numpy as jnp
import numpy as np
from jax import lax
from jax.experimental import pallas as pl
from jax.experimental.pallas import tpu as pltpu

D_MODEL = 1024
N_HEADS = 16
HEAD_DIM = D_MODEL // N_HEADS
D_FF = 4 * D_MODEL
CHUNK = 64
LEFT_CHUNKS = 8
LEFT_KEYS = LEFT_CHUNKS * CHUNK
REL_LEFT = 128
REL_RIGHT = CHUNK - 1
EPS = 1e-6
NEG_INF = -1e30
LOG2_E = math.log2(math.e)

V7X_LANES = 128
V7X_MXU_DIM = 256
HEADS_PER_LANE_TILE = V7X_LANES // HEAD_DIM
N_HEAD_PAIRS = N_HEADS // HEADS_PER_LANE_TILE

TOKEN_BLOCK = 512
ATTN_BLOCK = V7X_MXU_DIM
ATTN_A_WINDOW = LEFT_KEYS + ATTN_BLOCK
PIPE_UNROLL = 4
FF_CHUNK = 1024

F32 = jnp.float32
BF16 = jnp.bfloat16
MIB = 1024 * 1024


def _params(vmem_mib, n_grid):
    return pltpu.CompilerParams(
        dimension_semantics=("arbitrary",) * n_grid,
        vmem_limit_bytes=vmem_mib * MIB,
    )


def _resident(shape, index_map):
    return pl.BlockSpec(shape, index_map, pipeline_mode=pl.Buffered(1))


def _inv_rms(x):
    return lax.rsqrt(jnp.mean(x * x, axis=-1, keepdims=True) + EPS)


def _proj_kernel(x_ref, gq_ref, gkv_ref, wq_ref, wkt_ref, wv_ref, q_ref, kt_ref, v_ref, *,
                 q_scale, shared_norm):
    x = x_ref[...]
    xn = x * _inv_rms(x)
    n_kv = (xn * gkv_ref[...]).astype(BF16)
    n_q = n_kv if shared_norm else (xn * gq_ref[...]).astype(BF16)
    q = (jnp.dot(n_q, wq_ref[...], preferred_element_type=F32) * q_scale).astype(BF16)
    v = jnp.dot(n_kv, wv_ref[...], preferred_element_type=F32).astype(BF16)
    for p in range(N_HEAD_PAIRS):
        q_ref[p] = q[:, p * V7X_LANES:(p + 1) * V7X_LANES]
        v_ref[p] = v[:, p * V7X_LANES:(p + 1) * V7X_LANES]
    kt = lax.dot_general(wkt_ref[...], n_kv, (((1,), (1,)), ((), ())), preferred_element_type=F32)
    kt_ref[...] = kt.astype(BF16)


def _project(x, g_q, g_kv, wq, wkt, wv, *, q_scale, shared_norm):
    b, s, d = x.shape
    tm = TOKEN_BLOCK
    kern = functools.partial(_proj_kernel, q_scale=q_scale, shared_norm=shared_norm)
    wspec = _resident((d, d), lambda i, j: (0, 0))
    gspec = _resident((1, d), lambda i, j: (0, 0))
    return pl.pallas_call(
        kern,
        grid=(b, s // tm),
        in_specs=[pl.BlockSpec((None, tm, d), lambda i, j: (i, j, 0)), gspec, gspec, wspec, wspec, wspec],
        out_specs=[pl.BlockSpec((None, N_HEAD_PAIRS, tm, V7X_LANES), lambda i, j: (i, 0, j, 0)),
                   pl.BlockSpec((None, d, tm), lambda i, j: (i, 0, j)),
                   pl.BlockSpec((None, N_HEAD_PAIRS, tm, V7X_LANES), lambda i, j: (i, 0, j, 0))],
        out_shape=[jax.ShapeDtypeStruct((b, N_HEAD_PAIRS, s, V7X_LANES), BF16),
                   jax.ShapeDtypeStruct((b, d, s), BF16),
                   jax.ShapeDtypeStruct((b, N_HEAD_PAIRS, s, V7X_LANES), BF16)],
        compiler_params=_params(40, 2),
        name="proj",
    )(x, g_q, g_kv, wq, wkt, wv)


def _attn_a_kernel(q_ref, kt_ref, v_ref, tb_ref, o_ref, qh_ref, s_ref, p_ref, l_ref):
    tq, win = ATTN_BLOCK, ATTN_A_WINDOW
    heads = range(HEADS_PER_LANE_TILE)
    n_q = q_ref.shape[0] // tq
    lane = lax.broadcasted_iota(jnp.int32, (tq, V7X_LANES), 1)
    head_lanes = [(lane >= h * HEAD_DIM) & (lane < (h + 1) * HEAD_DIM) for h in heads]

    def rows(idx):
        return pl.ds(pl.multiple_of(idx * tq, tq), tq)

    def window(j):
        q0 = j * tq
        ws = jnp.maximum(q0 - LEFT_KEYS, 0)
        coff = pl.multiple_of(LEFT_KEYS - (q0 - ws), tq)
        return pl.ds(pl.multiple_of(ws, tq), win), pl.ds(coff, win)

    def mask_q(j, carry):
        q2 = q_ref[rows(j), :]
        for h in heads:
            qh_ref[h, rows(j), :] = jnp.where(head_lanes[h], q2, jnp.zeros_like(q2))
        return carry

    lax.fori_loop(0, n_q, mask_q, 0)

    def stage_a(j, slot):
        keys, cols = window(j)
        kt = kt_ref[:, keys]
        for h in heads:
            s_ref[slot, h] = (jnp.dot(qh_ref[h, rows(j), :], kt, preferred_element_type=F32)
                              + tb_ref[h, :, cols])

    def stage_b(slot):
        for h in heads:
            s = s_ref[slot, h]
            p = jnp.exp2(s - jnp.max(s, axis=-1, keepdims=True))
            l_ref[slot, h] = jnp.sum(p, axis=-1, keepdims=True)
            p_ref[slot, h] = p.astype(BF16)

    def stage_c(j, slot):
        keys, _ = window(j)
        v = v_ref[keys, :]
        outs = [jnp.dot(p_ref[slot, h], v, preferred_element_type=F32) / l_ref[slot, h] for h in heads]
        o_ref[rows(j), :] = jnp.where(lane < HEAD_DIM, outs[0], outs[1]).astype(BF16)

    def trip(t, carry):
        for u in range(2):
            i, in_b, in_c = carry
            new = jnp.minimum(i, n_q - 1)
            stage_c(in_c, u)
            stage_b(1 - u)
            stage_a(new, u)
            carry = (i + 1, new, in_b)
        return carry

    zero = jnp.int32(0)
    lax.fori_loop(0, pl.cdiv(n_q + 2, 2), trip, (zero, zero, zero))


def _bias_table(rel_bias):
    tq = ATTN_BLOCK
    n_col = LEFT_KEYS + ATTN_A_WINDOW
    period = n_col + tq
    m = np.arange(period)
    m = np.where(m < n_col, m, m - period)
    idx = np.clip(LEFT_KEYS - m, -REL_RIGHT, REL_LEFT) + REL_RIGHT
    w = rel_bias.astype(F32)[:, idx]
    skew = jnp.tile(w, (1, tq))[:, :tq * (period - 1)].reshape(-1, tq, period - 1)[:, :, :n_col]
    qi = np.arange(tq)[:, None]
    dk = np.arange(n_col)[None, :] - LEFT_KEYS
    chunk_gap = qi // CHUNK - np.floor_divide(dk, CHUNK)
    in_band = (chunk_gap >= 0) & (chunk_gap <= LEFT_CHUNKS)
    return jnp.where(in_band[None], skew * LOG2_E, NEG_INF)


def _attention_a(q, kt, v, table):
    b, _, s, _ = q.shape
    tq = ATTN_BLOCK
    hp = HEADS_PER_LANE_TILE
    return pl.pallas_call(
        _attn_a_kernel,
        grid=(b, N_HEAD_PAIRS),
        in_specs=[pl.BlockSpec((None, None, s, V7X_LANES), lambda i, p: (i, p, 0, 0)),
                  pl.BlockSpec((None, V7X_LANES, s), lambda i, p: (i, p, 0)),
                  pl.BlockSpec((None, None, s, V7X_LANES), lambda i, p: (i, p, 0, 0)),
                  pl.BlockSpec((hp, tq, table.shape[-1]), lambda i, p: (p, 0, 0))],
        out_specs=pl.BlockSpec((None, None, s, V7X_LANES), lambda i, p: (i, p, 0, 0)),
        out_shape=jax.ShapeDtypeStruct(q.shape, BF16),
        scratch_shapes=[pltpu.VMEM((hp, s, V7X_LANES), BF16),
                        pltpu.VMEM((2, hp, tq, ATTN_A_WINDOW), F32),
                        pltpu.VMEM((2, hp, tq, ATTN_A_WINDOW), BF16),
                        pltpu.VMEM((2, hp, tq, 1), F32)],
        compiler_params=_params(48, 2),
        name="attn_a",
    )(q, kt, v, table)


EXIT_LOG2 = -160.0
SOFTPLUS2_CLAMP = 126.0


def _softplus2(z):
    return jnp.log2(1.0 + jnp.exp2(z))


def _attn_b_kernel(q_ref, kt_ref, v_ref, ntri_ref, o_ref,
                   qh_ref, acc_ref, run_ref, d_ref, sp_ref, k0_ref, a_ref, active_ref, n_active_ref):
    t = ATTN_BLOCK
    heads = range(HEADS_PER_LANE_TILE)
    n_q = q_ref.shape[0] // t
    spare = n_q
    lane = lax.broadcasted_iota(jnp.int32, (t, V7X_LANES), 1)
    head_lanes = [(lane >= h * HEAD_DIM) & (lane < (h + 1) * HEAD_DIM) for h in heads]
    causal = (lax.broadcasted_iota(jnp.int32, (t, t), 1) < lax.broadcasted_iota(jnp.int32, (t, t), 0))

    def rows(idx):
        return pl.ds(pl.multiple_of(idx * t, t), t)

    def mask_q(j, carry):
        q2 = q_ref[rows(j), :]
        for h in heads:
            qh_ref[h, rows(j), :] = jnp.where(head_lanes[h], q2, jnp.zeros_like(q2))
        return carry

    lax.fori_loop(0, n_q, mask_q, 0)

    def stage_ab(item, slot, diagonal):
        j, k = item
        kt = kt_ref[:, rows(k)]
        for h in heads:
            qh = qh_ref[h, rows(jnp.minimum(j, n_q - 1)), :]
            z = jnp.minimum(jnp.dot(qh, kt, preferred_element_type=F32), SOFTPLUS2_CLAMP)
            sp = _softplus2(z)
            d = z - sp
            if diagonal:
                sp = jnp.where(causal, sp, 0.0)
                d = jnp.where(causal, d, NEG_INF)
            d_ref[slot, h] = d
            sp_ref[slot, h] = sp.astype(BF16)
            k0_ref[slot, h] = sp[:, 0:1] if diagonal else z[:, 0:1]

    def stage_cd(item, slot, diagonal):
        j, _ = item
        ntri = ntri_ref[...]
        tile_max = None
        for h in heads:
            later = jnp.dot(sp_ref[slot, h], ntri, preferred_element_type=F32)
            if diagonal:
                a_ref[slot, h] = jnp.exp2(d_ref[slot, h] + later).astype(BF16)
                run_ref[j, h] = later[:, 0:1] - k0_ref[slot, h]
            else:
                log2_a = d_ref[slot, h] + later + run_ref[j, h]
                a_ref[slot, h] = jnp.exp2(log2_a).astype(BF16)
                run_new = log2_a[:, 0:1] - k0_ref[slot, h]
                run_ref[j, h] = run_new
                head_max = jnp.max(run_new)
                tile_max = head_max if tile_max is None else jnp.maximum(tile_max, head_max)
        if not diagonal:
            finished = tile_max <= EXIT_LOG2
            was_active = active_ref[j]
            active_ref[j] = jnp.where(finished, 0, was_active)
            n_active_ref[0] = n_active_ref[0] - jnp.where(finished, was_active, 0)

    def stage_e(item, slot, diagonal):
        j, k = item
        v = v_ref[rows(k), :]
        for h in heads:
            pv = jnp.dot(a_ref[slot, h], v, preferred_element_type=F32)
            acc_ref[j, h] = pv if diagonal else acc_ref[j, h] + pv

    def pipeline_step(slot, new, in_cd, in_e, diagonal):
        diag_ab, diag_cd, diag_e = diagonal
        stage_e(in_e, slot, diag_e)
        stage_cd(in_cd, 1 - slot, diag_cd)
        stage_ab(new, slot, diag_ab)

    idle = (jnp.int32(spare), jnp.int32(0))
    n_stages = 3

    def mark_active(j, carry):
        active_ref[j] = jnp.where((j >= 1) & (j < n_q), 1, 0)
        return carry

    lax.fori_loop(0, n_q + 1, mark_active, 0)
    n_active_ref[0] = n_q - 1

    def diagonal_trip(trip, carry):
        for u in range(PIPE_UNROLL):
            p, in_cd, in_e = carry
            real = p < n_q
            new = (jnp.where(real, p, spare), jnp.where(real, p, 0))
            pipeline_step(u % 2, new, in_cd, in_e, (True, True, True))
            carry = (p + 1, new, in_cd)
        return carry

    _, in_cd, in_e = lax.fori_loop(0, pl.cdiv(n_q, PIPE_UNROLL), diagonal_trip, (jnp.int32(0), idle, idle))

    first_items = [(j, j - gap) for gap in range(1, n_q) for j in range(gap, n_q)][:PIPE_UNROLL + 1]
    for u in range(PIPE_UNROLL):
        new = (jnp.int32(first_items[u][0]), jnp.int32(first_items[u][1])) if u < len(first_items) else idle
        pipeline_step(u % 2, new, in_cd, in_e, (False, u < 1, u < 2))
        in_cd, in_e = new, in_cd
    if len(first_items) > PIPE_UNROLL:
        j_next, k_next = first_items[PIPE_UNROLL]
        cursor = (jnp.int32(j_next), jnp.int32(j_next - k_next))
    else:
        cursor = (jnp.int32(n_q - 1), jnp.int32(n_q))
    exhausted = (jnp.int32(n_q - 1), jnp.int32(n_q))

    def advance(cur):
        j, gap = cur
        wrap = j >= n_q - 1
        return jnp.where(wrap, gap + 1, j + 1), jnp.where(wrap, gap + 1, gap)

    def next_active(cur):
        return lax.while_loop(lambda c: (c[1] < n_q) & (active_ref[jnp.minimum(c[0], spare)] == 0), advance, cur)

    def off_diagonal_trip(carry):
        cur, in_cd, in_e, cooldown = carry
        none_active = n_active_ref[0] == 0
        cur = tuple(jnp.where(none_active, e, c) for e, c in zip(exhausted, cur))
        items = []
        for u in range(PIPE_UNROLL):
            cur = next_active(cur)
            items.append(cur)
            cur = advance(cur)
        for u in range(PIPE_UNROLL):
            j, gap = items[u]
            real = gap < n_q
            new = (jnp.where(real, j, spare), jnp.where(real, j - gap, 0))
            pipeline_step(u % 2, new, in_cd, in_e, (False, False, False))
            cooldown = jnp.where(real, n_stages - 1, jnp.maximum(cooldown - 1, 0))
            in_cd, in_e = new, in_cd
        return cur, in_cd, in_e, cooldown

    lax.while_loop(lambda carry: carry[-1] > 0, off_diagonal_trip,
                   (cursor, in_cd, in_e, jnp.int32(n_stages - 1)))

    def write_out(j, carry):
        o_ref[rows(j), :] = jnp.where(lane < HEAD_DIM, acc_ref[j, 0], acc_ref[j, 1]).astype(BF16)
        return carry

    lax.fori_loop(0, n_q, write_out, 0)


def _attention_b(q, kt, v):
    b, _, s, _ = q.shape
    t = ATTN_BLOCK
    hp = HEADS_PER_LANE_TILE
    n_state = s // t + 1
    row = jnp.arange(t, dtype=jnp.int32)[:, None]
    col = jnp.arange(t, dtype=jnp.int32)[None, :]
    ntri = jnp.where(row > col, -1.0, 0.0).astype(BF16)
    return pl.pallas_call(
        _attn_b_kernel,
        grid=(b, N_HEAD_PAIRS),
        in_specs=[pl.BlockSpec((None, None, s, V7X_LANES), lambda i, p: (i, p, 0, 0)),
                  pl.BlockSpec((None, V7X_LANES, s), lambda i, p: (i, p, 0)),
                  pl.BlockSpec((None, None, s, V7X_LANES), lambda i, p: (i, p, 0, 0)),
                  _resident((t, t), lambda i, p: (0, 0))],
        out_specs=pl.BlockSpec((None, None, s, V7X_LANES), lambda i, p: (i, p, 0, 0)),
        out_shape=jax.ShapeDtypeStruct(q.shape, BF16),
        scratch_shapes=[pltpu.VMEM((hp, s, V7X_LANES), BF16),
                        pltpu.VMEM((n_state, hp, t, V7X_LANES), F32),
                        pltpu.VMEM((n_state, hp, t, 1), F32),
                        pltpu.VMEM((2, hp, t, t), F32),
                        pltpu.VMEM((2, hp, t, t), BF16),
                        pltpu.VMEM((2, hp, t, 1), F32),
                        pltpu.VMEM((2, hp, t, t), BF16),
                        pltpu.SMEM((n_state,), jnp.int32),
                        pltpu.SMEM((1,), jnp.int32)],
        compiler_params=_params(48, 2),
        name="attn_b",
    )(q, kt, v, ntri)


def _mlp_kernel(h_ref, a_ref, wo_ref, g_ref, wup_ref, wdn_ref, gf_ref, o_ref, *, final_norm):
    attn = jnp.concatenate([a_ref[p] for p in range(N_HEAD_PAIRS)], axis=-1)
    h1 = h_ref[...] + jnp.dot(attn, wo_ref[...], preferred_element_type=F32)
    n = (h1 * _inv_rms(h1) * g_ref[...]).astype(BF16)
    acc = h1
    for c in range(D_FF // FF_CHUNK):
        u = jnp.dot(n, wup_ref[:, c * FF_CHUNK:(c + 1) * FF_CHUNK], preferred_element_type=F32)
        r = jnp.maximum(u, 0.0)
        acc = acc + jnp.dot((r * r).astype(BF16), wdn_ref[c * FF_CHUNK:(c + 1) * FF_CHUNK, :],
                            preferred_element_type=F32)
    if final_norm:
        acc = acc * _inv_rms(acc) * gf_ref[...]
    o_ref[...] = acc


def _mlp(h, attn, wo, g, wup, wdn, g_final, *, final_norm):
    n_tok, d = h.shape
    tm = TOKEN_BLOCK
    blocks_per_seq = attn.shape[2] // tm
    kern = functools.partial(_mlp_kernel, final_norm=final_norm)
    gspec = _resident((1, d), lambda i: (0, 0))
    return pl.pallas_call(
        kern,
        grid=(n_tok // tm,),
        in_specs=[pl.BlockSpec((tm, d), lambda i: (i, 0)),
                  pl.BlockSpec((None, N_HEAD_PAIRS, tm, V7X_LANES),
                               lambda i: (i // blocks_per_seq, 0, i % blocks_per_seq, 0)),
                  _resident((d, d), lambda i: (0, 0)),
                  gspec,
                  _resident((d, D_FF), lambda i: (0, 0)),
                  _resident((D_FF, d), lambda i: (0, 0)),
                  gspec],
        out_specs=pl.BlockSpec((tm, d), lambda i: (i, 0)),
        out_shape=jax.ShapeDtypeStruct((n_tok, d), F32),
        compiler_params=_params(56, 1),
        name="mlp_final" if final_norm else "mlp",
    )(h, attn, wo, g, wup, wdn, g_final)


def kernel(x, a_norm_attn, a_w_qkv, a_rel_bias, a_w_o, a_norm_mlp, a_w_up, a_w_down, kv_norm, w_kv,
           b_norm_attn, b_w_q, b_w_o, b_norm_mlp, b_w_up, b_w_down, final_norm):
    b, s, d = x.shape
    assert (s % TOKEN_BLOCK, s % ATTN_BLOCK, d) == (0, 0, D_MODEL) and s >= ATTN_A_WINDOW
    assert a_norm_attn.shape[0] == 1 and b_norm_attn.shape[0] == 1
    scale = 1.0 / math.sqrt(HEAD_DIM)
    row = lambda g: g.reshape(1, d).astype(F32)

    wq, wk, wv = (a_w_qkv[0][:, i * d:(i + 1) * d] for i in range(3))
    g_a = row(a_norm_attn[0])
    q, kt, v = _project(x, g_a, g_a, wq.astype(BF16), wk.T.astype(BF16), wv.astype(BF16),
                        q_scale=scale * LOG2_E, shared_norm=True)
    attn = _attention_a(q, kt, v, _bias_table(a_rel_bias[0]))
    h = _mlp(x.reshape(b * s, d), attn, a_w_o[0].astype(BF16), row(a_norm_mlp[0]),
             a_w_up[0].astype(BF16), a_w_down[0].astype(BF16), row(final_norm), final_norm=False)

    q, kt, v = _project(h.reshape(b, s, d), row(b_norm_attn[0]), row(kv_norm),
                        b_w_q[0].astype(BF16), w_kv[:, :d].T.astype(BF16), w_kv[:, d:].astype(BF16),
                        q_scale=scale * LOG2_E, shared_norm=False)
    attn = _attention_b(q, kt, v)
    out = _mlp(h, attn, b_w_o[0].astype(BF16), row(b_norm_mlp[0]),
               b_w_up[0].astype(BF16), b_w_down[0].astype(BF16), row(final_norm), final_norm=True)
    return out.reshape(b, s, d)
```

```python
import functools
import math

import jax
import jax.numpy as jnp
import numpy as np
from jax import lax
from jax.experimental import pallas as pl
from jax.experimental.pallas import tpu as pltpu

D_MODEL = 1024
N_HEADS = 16
HEAD_DIM = D_MODEL // N_HEADS
D_FF = 4 * D_MODEL
CHUNK = 64
LEFT_CHUNKS = 8
LEFT_KEYS = LEFT_CHUNKS * CHUNK
REL_LEFT = 128
REL_RIGHT = CHUNK - 1
EPS = 1e-6
NEG_INF = -1e30
LOG2_E = math.log2(math.e)

V7X_LANES = 128
V7X_MXU_DIM = 256
HEADS_PER_LANE_TILE = V7X_LANES // HEAD_DIM
N_HEAD_PAIRS = N_HEADS // HEADS_PER_LANE_TILE

TOKEN_BLOCK = 512
ATTN_BLOCK = V7X_MXU_DIM
ATTN_A_WINDOW = LEFT_KEYS + ATTN_BLOCK
PIPE_UNROLL = 4
FF_CHUNK = 1024

F32 = jnp.float32
BF16 = jnp.bfloat16
MIB = 1024 * 1024


def _params(vmem_mib, n_grid):
    return pltpu.CompilerParams(
        dimension_semantics=("arbitrary",) * n_grid,
        vmem_limit_bytes=vmem_mib * MIB,
    )


def _resident(shape, index_map):
    return pl.BlockSpec(shape, index_map, pipeline_mode=pl.Buffered(1))


def _inv_rms(x):
    return lax.rsqrt(jnp.mean(x * x, axis=-1, keepdims=True) + EPS)


def _proj_kernel(x_ref, gq_ref, gkv_ref, wq_ref, wkt_ref, wv_ref, q_ref, kt_ref, v_ref, *,
                 q_scale, shared_norm):
    x = x_ref[...]
    xn = x * _inv_rms(x)
    n_kv = (xn * gkv_ref[...]).astype(BF16)
    n_q = n_kv if shared_norm else (xn * gq_ref[...]).astype(BF16)
    q = (jnp.dot(n_q, wq_ref[...], preferred_element_type=F32) * q_scale).astype(BF16)
    v = jnp.dot(n_kv, wv_ref[...], preferred_element_type=F32).astype(BF16)
    for p in range(N_HEAD_PAIRS):
        q_ref[p] = q[:, p * V7X_LANES:(p + 1) * V7X_LANES]
        v_ref[p] = v[:, p * V7X_LANES:(p + 1) * V7X_LANES]
    kt = lax.dot_general(wkt_ref[...], n_kv, (((1,), (1,)), ((), ())), preferred_element_type=F32)
    kt_ref[...] = kt.astype(BF16)


def _project(x, g_q, g_kv, wq, wkt, wv, *, q_scale, shared_norm):
    b, s, d = x.shape
    tm = TOKEN_BLOCK
    kern = functools.partial(_proj_kernel, q_scale=q_scale, shared_norm=shared_norm)
    wspec = _resident((d, d), lambda i, j: (0, 0))
    gspec = _resident((1, d), lambda i, j: (0, 0))
    return pl.pallas_call(
        kern,
        grid=(b, s // tm),
        in_specs=[pl.BlockSpec((None, tm, d), lambda i, j: (i, j, 0)), gspec, gspec, wspec, wspec, wspec],
        out_specs=[pl.BlockSpec((None, N_HEAD_PAIRS, tm, V7X_LANES), lambda i, j: (i, 0, j, 0)),
                   pl.BlockSpec((None, d, tm), lambda i, j: (i, 0, j)),
                   pl.BlockSpec((None, N_HEAD_PAIRS, tm, V7X_LANES), lambda i, j: (i, 0, j, 0))],
        out_shape=[jax.ShapeDtypeStruct((b, N_HEAD_PAIRS, s, V7X_LANES), BF16),
                   jax.ShapeDtypeStruct((b, d, s), BF16),
                   jax.ShapeDtypeStruct((b, N_HEAD_PAIRS, s, V7X_LANES), BF16)],
        compiler_params=_params(40, 2),
        name="proj",
    )(x, g_q, g_kv, wq, wkt, wv)


def _attn_a_kernel(q_ref, kt_ref, v_ref, tb_ref, o_ref, qh_ref, s_ref):
    tq, win = ATTN_BLOCK, ATTN_A_WINDOW
    heads = range(HEADS_PER_LANE_TILE)
    n_q = q_ref.shape[0] // tq
    lane = lax.broadcasted_iota(jnp.int32, (tq, V7X_LANES), 1)
    head_lanes = [(lane >= h * HEAD_DIM) & (lane < (h + 1) * HEAD_DIM) for h in heads]

    def rows(idx):
        return pl.ds(pl.multiple_of(idx * tq, tq), tq)

    def window(j):
        q0 = j * tq
        ws = jnp.maximum(q0 - LEFT_KEYS, 0)
        coff = pl.multiple_of(LEFT_KEYS - (q0 - ws), tq)
        return pl.ds(pl.multiple_of(ws, tq), win), pl.ds(coff, win)

    def mask_q(j, carry):
        q2 = q_ref[rows(j), :]
        for h in heads:
            qh_ref[h, rows(j), :] = jnp.where(head_lanes[h], q2, jnp.zeros_like(q2))
        return carry

    lax.fori_loop(0, n_q, mask_q, 0)

    def stage_a(j, slot):
        keys, cols = window(j)
        kt = kt_ref[:, keys]
        for h in heads:
            s_ref[slot, h] = (jnp.dot(qh_ref[h, rows(j), :], kt, preferred_element_type=F32)
                              + tb_ref[h, :, cols])

    def stage_bc(j, slot):
        keys, _ = window(j)
        v = v_ref[keys, :]
        outs = []
        for h in heads:
            s = s_ref[slot, h]
            p = jnp.exp2(s - jnp.max(s, axis=-1, keepdims=True))
            l = jnp.sum(p, axis=-1, keepdims=True)
            outs.append(jnp.dot(p.astype(BF16), v, preferred_element_type=F32) / l)
        o_ref[rows(j), :] = jnp.where(lane < HEAD_DIM, outs[0], outs[1]).astype(BF16)

    def trip(t, carry):
        for u in range(PIPE_UNROLL):
            i, in_bc = carry
            new = jnp.minimum(i, n_q - 1)
            stage_a(new, u % 2)
            stage_bc(in_bc, 1 - u % 2)
            carry = (i + 1, new)
        return carry

    zero = jnp.int32(0)
    lax.fori_loop(0, pl.cdiv(n_q + 1, PIPE_UNROLL), trip, (zero, zero))


def _bias_table(rel_bias):
    tq = ATTN_BLOCK
    n_col = LEFT_KEYS + ATTN_A_WINDOW
    period = n_col + tq
    m = np.arange(period)
    m = np.where(m < n_col, m, m - period)
    idx = np.clip(LEFT_KEYS - m, -REL_RIGHT, REL_LEFT) + REL_RIGHT
    w = rel_bias.astype(F32)[:, idx]
    skew = jnp.tile(w, (1, tq))[:, :tq * (period - 1)].reshape(-1, tq, period - 1)[:, :, :n_col]
    qi = np.arange(tq)[:, None]
    dk = np.arange(n_col)[None, :] - LEFT_KEYS
    chunk_gap = qi // CHUNK - np.floor_divide(dk, CHUNK)
    in_band = (chunk_gap >= 0) & (chunk_gap <= LEFT_CHUNKS)
    return jnp.where(in_band[None], skew * LOG2_E, NEG_INF)


def _attention_a(q, kt, v, table):
    b, _, s, _ = q.shape
    tq = ATTN_BLOCK
    hp = HEADS_PER_LANE_TILE
    return pl.pallas_call(
        _attn_a_kernel,
        grid=(b, N_HEAD_PAIRS),
        in_specs=[pl.BlockSpec((None, None, s, V7X_LANES), lambda i, p: (i, p, 0, 0)),
                  pl.BlockSpec((None, V7X_LANES, s), lambda i, p: (i, p, 0)),
                  pl.BlockSpec((None, None, s, V7X_LANES), lambda i, p: (i, p, 0, 0)),
                  pl.BlockSpec((hp, tq, table.shape[-1]), lambda i, p: (p, 0, 0))],
        out_specs=pl.BlockSpec((None, None, s, V7X_LANES), lambda i, p: (i, p, 0, 0)),
        out_shape=jax.ShapeDtypeStruct(q.shape, BF16),
        scratch_shapes=[pltpu.VMEM((hp, s, V7X_LANES), BF16),
                        pltpu.VMEM((2, hp, tq, ATTN_A_WINDOW), F32)],
        compiler_params=_params(48, 2),
        name="attn_a",
    )(q, kt, v, table)


EXIT_LOG2 = -160.0
SOFTPLUS2_CLAMP = 126.0


def _softplus2(z):
    return jnp.log2(1.0 + jnp.exp2(z))


def _attn_b_kernel(q_ref, kt_ref, v_ref, ntri_ref, o_ref,
                   qh_ref, acc_ref, run_ref, d_ref, sp_ref, k0_ref, a_ref, active_ref, n_active_ref):
    t = ATTN_BLOCK
    heads = range(HEADS_PER_LANE_TILE)
    n_q = q_ref.shape[0] // t
    spare = n_q
    lane = lax.broadcasted_iota(jnp.int32, (t, V7X_LANES), 1)
    head_lanes = [(lane >= h * HEAD_DIM) & (lane < (h + 1) * HEAD_DIM) for h in heads]
    causal = (lax.broadcasted_iota(jnp.int32, (t, t), 1) < lax.broadcasted_iota(jnp.int32, (t, t), 0))

    def rows(idx):
        return pl.ds(pl.multiple_of(idx * t, t), t)

    def mask_q(j, carry):
        q2 = q_ref[rows(j), :]
        for h in heads:
            qh_ref[h, rows(j), :] = jnp.where(head_lanes[h], q2, jnp.zeros_like(q2))
        return carry

    lax.fori_loop(0, n_q, mask_q, 0)

    def stage_ab(item, slot, diagonal):
        j, k = item
        kt = kt_ref[:, rows(k)]
        for h in heads:
            qh = qh_ref[h, rows(jnp.minimum(j, n_q - 1)), :]
            z = jnp.minimum(jnp.dot(qh, kt, preferred_element_type=F32), SOFTPLUS2_CLAMP)
            sp = _softplus2(z)
            d = z - sp
            if diagonal:
                sp = jnp.where(causal, sp, 0.0)
                d = jnp.where(causal, d, NEG_INF)
            d_ref[slot, h] = d
            sp_ref[slot, h] = sp.astype(BF16)
            k0_ref[slot, h] = sp[:, 0:1] if diagonal else z[:, 0:1]

    def stage_cd(item, slot, diagonal):
        j, _ = item
        ntri = ntri_ref[...]
        tile_max = None
        for h in heads:
            later = jnp.dot(sp_ref[slot, h], ntri, preferred_element_type=F32)
            if diagonal:
                a_ref[slot, h] = jnp.exp2(d_ref[slot, h] + later).astype(BF16)
                run_ref[j, h] = later[:, 0:1] - k0_ref[slot, h]
            else:
                log2_a = d_ref[slot, h] + later + run_ref[j, h]
                a_ref[slot, h] = jnp.exp2(log2_a).astype(BF16)
                run_new = log2_a[:, 0:1] - k0_ref[slot, h]
                run_ref[j, h] = run_new
                head_max = jnp.max(run_new)
                tile_max = head_max if tile_max is None else jnp.maximum(tile_max, head_max)
        if not diagonal:
            finished = tile_max <= EXIT_LOG2
            was_active = active_ref[j]
            active_ref[j] = jnp.where(finished, 0, was_active)
            n_active_ref[0] = n_active_ref[0] - jnp.where(finished, was_active, 0)

    def stage_e(item, slot, diagonal):
        j, k = item
        v = v_ref[rows(k), :]
        for h in heads:
            pv = jnp.dot(a_ref[slot, h], v, preferred_element_type=F32)
            acc_ref[j, h] = pv if diagonal else acc_ref[j, h] + pv

    def pipeline_step(slot, new, in_cd, in_e, diagonal):
        diag_ab, diag_cd, diag_e = diagonal
        stage_e(in_e, slot, diag_e)
        stage_cd(in_cd, 1 - slot, diag_cd)
        stage_ab(new, slot, diag_ab)

    idle = (jnp.int32(spare), jnp.int32(0))
    n_stages = 3

    def mark_active(j, carry):
        active_ref[j] = jnp.where((j >= 1) & (j < n_q), 1, 0)
        return carry

    lax.fori_loop(0, n_q + 1, mark_active, 0)
    n_active_ref[0] = n_q - 1

    def diagonal_trip(trip, carry):
        for u in range(PIPE_UNROLL):
            p, in_cd, in_e = carry
            real = p < n_q
            new = (jnp.where(real, p, spare), jnp.where(real, p, 0))
            pipeline_step(u % 2, new, in_cd, in_e, (True, True, True))
            carry = (p + 1, new, in_cd)
        return carry

    _, in_cd, in_e = lax.fori_loop(0, pl.cdiv(n_q, PIPE_UNROLL), diagonal_trip, (jnp.int32(0), idle, idle))

    first_items = [(j, j - gap) for gap in range(1, n_q) for j in range(gap, n_q)][:PIPE_UNROLL + 1]
    for u in range(PIPE_UNROLL):
        new = (jnp.int32(first_items[u][0]), jnp.int32(first_items[u][1])) if u < len(first_items) else idle
        pipeline_step(u % 2, new, in_cd, in_e, (False, u < 1, u < 2))
        in_cd, in_e = new, in_cd
    if len(first_items) > PIPE_UNROLL:
        j_next, k_next = first_items[PIPE_UNROLL]
        cursor = (jnp.int32(j_next), jnp.int32(j_next - k_next))
    else:
        cursor = (jnp.int32(n_q - 1), jnp.int32(n_q))
    exhausted = (jnp.int32(n_q - 1), jnp.int32(n_q))

    def advance(cur):
        j, gap = cur
        wrap = j >= n_q - 1
        return jnp.where(wrap, gap + 1, j + 1), jnp.where(wrap, gap + 1, gap)

    def next_active(cur):
        return lax.while_loop(lambda c: (c[1] < n_q) & (active_ref[jnp.minimum(c[0], spare)] == 0), advance, cur)

    def off_diagonal_trip(carry):
        cur, in_cd, in_e, cooldown = carry
        none_active = n_active_ref[0] == 0
        cur = tuple(jnp.where(none_active, e, c) for e, c in zip(exhausted, cur))
        items = []
        for u in range(PIPE_UNROLL):
            cur = next_active(cur)
            items.append(cur)
            cur = advance(cur)
        for u in range(PIPE_UNROLL):
            j, gap = items[u]
            real = gap < n_q
            new = (jnp.where(real, j, spare), jnp.where(real, j - gap, 0))
            pipeline_step(u % 2, new, in_cd, in_e, (False, False, False))
            cooldown = jnp.where(real, n_stages - 1, jnp.maximum(cooldown - 1, 0))
            in_cd, in_e = new, in_cd
        return cur, in_cd, in_e, cooldown

    lax.while_loop(lambda carry: carry[-1] > 0, off_diagonal_trip,
                   (cursor, in_cd, in_e, jnp.int32(n_stages - 1)))

    def write_out(j, carry):
        o_ref[rows(j), :] = jnp.where(lane < HEAD_DIM, acc_ref[j, 0], acc_ref[j, 1]).astype(BF16)
        return carry

    lax.fori_loop(0, n_q, write_out, 0)


def _attention_b(q, kt, v):
    b, _, s, _ = q.shape
    t = ATTN_BLOCK
    hp = HEADS_PER_LANE_TILE
    n_state = s // t + 1
    row = jnp.arange(t, dtype=jnp.int32)[:, None]
    col = jnp.arange(t, dtype=jnp.int32)[None, :]
    ntri = jnp.where(row > col, -1.0, 0.0).astype(BF16)
    return pl.pallas_call(
        _attn_b_kernel,
        grid=(b, N_HEAD_PAIRS),
        in_specs=[pl.BlockSpec((None, None, s, V7X_LANES), lambda i, p: (i, p, 0, 0)),
                  pl.BlockSpec((None, V7X_LANES, s), lambda i, p: (i, p, 0)),
                  pl.BlockSpec((None, None, s, V7X_LANES), lambda i, p: (i, p, 0, 0)),
                  _resident((t, t), lambda i, p: (0, 0))],
        out_specs=pl.BlockSpec((None, None, s, V7X_LANES), lambda i, p: (i, p, 0, 0)),
        out_shape=jax.ShapeDtypeStruct(q.shape, BF16),
        scratch_shapes=[pltpu.VMEM((hp, s, V7X_LANES), BF16),
                        pltpu.VMEM((n_state, hp, t, V7X_LANES), F32),
                        pltpu.VMEM((n_state, hp, t, 1), F32),
                        pltpu.VMEM((2, hp, t, t), F32),
                        pltpu.VMEM((2, hp, t, t), BF16),
                        pltpu.VMEM((2, hp, t, 1), F32),
                        pltpu.VMEM((2, hp, t, t), BF16),
                        pltpu.SMEM((n_state,), jnp.int32),
                        pltpu.SMEM((1,), jnp.int32)],
        compiler_params=_params(48, 2),
        name="attn_b",
    )(q, kt, v, ntri)


def _mlp_kernel(h_ref, a_ref, wo_ref, g_ref, wup_ref, wdn_ref, gf_ref, o_ref, *, final_norm):
    attn = jnp.concatenate([a_ref[p] for p in range(N_HEAD_PAIRS)], axis=-1)
    h1 = h_ref[...] + jnp.dot(attn, wo_ref[...], preferred_element_type=F32)
    n = (h1 * _inv_rms(h1) * g_ref[...]).astype(BF16)
    acc = h1
    for c in range(D_FF // FF_CHUNK):
        u = jnp.dot(n, wup_ref[:, c * FF_CHUNK:(c + 1) * FF_CHUNK], preferred_element_type=F32)
        r = jnp.maximum(u, 0.0)
        acc = acc + jnp.dot((r * r).astype(BF16), wdn_ref[c * FF_CHUNK:(c + 1) * FF_CHUNK, :],
                            preferred_element_type=F32)
    if final_norm:
        acc = acc * _inv_rms(acc) * gf_ref[...]
    o_ref[...] = acc


def _mlp(h, attn, wo, g, wup, wdn, g_final, *, final_norm):
    n_tok, d = h.shape
    tm = TOKEN_BLOCK
    blocks_per_seq = attn.shape[2] // tm
    kern = functools.partial(_mlp_kernel, final_norm=final_norm)
    gspec = _resident((1, d), lambda i: (0, 0))
    return pl.pallas_call(
        kern,
        grid=(n_tok // tm,),
        in_specs=[pl.BlockSpec((tm, d), lambda i: (i, 0)),
                  pl.BlockSpec((None, N_HEAD_PAIRS, tm, V7X_LANES),
                               lambda i: (i // blocks_per_seq, 0, i % blocks_per_seq, 0)),
                  _resident((d, d), lambda i: (0, 0)),
                  gspec,
                  _resident((d, D_FF), lambda i: (0, 0)),
                  _resident((D_FF, d), lambda i: (0, 0)),
                  gspec],
        out_specs=pl.BlockSpec((tm, d), lambda i: (i, 0)),
        out_shape=jax.ShapeDtypeStruct((n_tok, d), F32),
        compiler_params=_params(56, 1),
        name="mlp_final" if final_norm else "mlp",
    )(h, attn, wo, g, wup, wdn, g_final)


def kernel(x, a_norm_attn, a_w_qkv, a_rel_bias, a_w_o, a_norm_mlp, a_w_up, a_w_down, kv_norm, w_kv,
           b_norm_attn, b_w_q, b_w_o, b_norm_mlp, b_w_up, b_w_down, final_norm):
    b, s, d = x.shape
    assert (s % TOKEN_BLOCK, s % ATTN_BLOCK, d) == (0, 0, D_MODEL) and s >= ATTN_A_WINDOW
    assert a_norm_attn.shape[0] == 1 and b_norm_attn.shape[0] == 1
    scale = 1.0 / math.sqrt(HEAD_DIM)
    row = lambda g: g.reshape(1, d).astype(F32)

    wq, wk, wv = (a_w_qkv[0][:, i * d:(i + 1) * d] for i in range(3))
    g_a = row(a_norm_attn[0])
    q, kt, v = _project(x, g_a, g_a, wq.astype(BF16), wk.T.astype(BF16), wv.astype(BF16),
                        q_scale=scale * LOG2_E, shared_norm=True)
    attn = _attention_a(q, kt, v, _bias_table(a_rel_bias[0]))
    h = _mlp(x.reshape(b * s, d), attn, a_w_o[0].astype(BF16), row(a_norm_mlp[0]),
             a_w_up[0].astype(BF16), a_w_down[0].astype(BF16), row(final_norm), final_norm=False)

    q, kt, v = _project(h.reshape(b, s, d), row(b_norm_attn[0]), row(kv_norm),
                        b_w_q[0].astype(BF16), w_kv[:, :d].T.astype(BF16), w_kv[:, d:].astype(BF16),
                        q_scale=scale * LOG2_E, shared_norm=False)
    attn = _attention_b(q, kt, v)
    out = _mlp(h, attn, b_w_o[0].astype(BF16), row(b_norm_mlp[0]),
               b_w_up[0].astype(BF16), b_w_down[0].astype(BF16), row(final_norm), final_norm=True)
    return out.reshape(b, s, d)
```
